```python
import math
import jax, jax.numpy as jnp
from jax import lax
import numpy as np

D_MODEL = 1024
BATCH = 8
SEQ = 4096
DEPTH = 1

GRID_W = 64
DILATED_GROUPS = ((128, 1), (512, 4), (2048, 16))
N_GROUPS_A = len(DILATED_GROUPS)
HEADS_PER_GROUP_A = 8
HEAD_DIM_A = 64
N_HEADS_A = N_GROUPS_A * HEADS_PER_GROUP_A
GROUP_WIDTH_A = HEADS_PER_GROUP_A * HEAD_DIM_A
A_QKV_WIDTH = 3 * N_HEADS_A * HEAD_DIM_A
BAND_BLK = 64
N_HEADS_B = 8
N_KV_B = 2
GQA_GROUP_B = N_HEADS_B // N_KV_B
HEAD_DIM_B = 128
B_Q_WIDTH = N_HEADS_B * HEAD_DIM_B
B_KV_WIDTH = N_KV_B * HEAD_DIM_B
ROPE_THETA = 10000.0
Q_BLOCK = 128
N_BRANCHES = 2
GATE_WIDTH = N_BRANCHES * D_MODEL
IN_WIDTH = A_QKV_WIDTH + B_Q_WIDTH + 2 * B_KV_WIDTH + GATE_WIDTH
N_BUCKETS = 32
MAX_DISTANCE = 1024
D_FF = 2816
EPS = 1e-6
NEG_INF = -1e30

kernel_name = 'hybrid_dilated_axial_gqa_macaron'


def rmsnorm(x, g):
    xf = x.astype(jnp.float32)
    y = xf * lax.rsqrt(jnp.mean(xf * xf, axis=-1, keepdims=True) + EPS)
    return (y * g.astype(jnp.float32)).astype(x.dtype)


def swiglu(x, w1, w3, w2):
    return (jax.nn.silu(x @ w1) * (x @ w3)) @ w2


def t5_bucket(rel):
    n = N_BUCKETS // 2
    max_exact = n // 2
    ret = jnp.where(rel > 0, n, 0)
    a = jnp.abs(rel)
    af = jnp.maximum(a, 1).astype(jnp.float32)
    large = max_exact + (jnp.log(af / max_exact) / math.log(MAX_DISTANCE / max_exact)
                         * (n - max_exact)).astype(jnp.int32)
    large = jnp.minimum(large, n - 1)
    return ret + jnp.where(a < max_exact, a, large)


def dilated_group_attention(q, k, v, bias_tab, dilation, half):
    B, S, H, hd = q.shape
    L = S // dilation
    nblk = -(-L // BAND_BLK)
    Lp = nblk * BAND_BLK

    def to_sub(a):
        a = a.reshape(B, L, dilation, H, hd).transpose(0, 2, 1, 3, 4)
        return jnp.pad(a, ((0, 0), (0, 0), (0, Lp - L), (0, 0), (0, 0)))

    def band(a):
        a = jnp.pad(a, ((0, 0), (0, 0), (BAND_BLK, BAND_BLK), (0, 0), (0, 0)))
        blocks = a.reshape(B, dilation, nblk + 2, BAND_BLK, H, hd)
        return jnp.concatenate([blocks[:, :, :-2], blocks[:, :, 1:-1], blocks[:, :, 2:]], axis=3)

    qs = to_sub(q).reshape(B, dilation, nblk, BAND_BLK, H, hd)
    kb = band(to_sub(k))
    vb = band(to_sub(v))

    scores = jnp.einsum('brnqhd,brnkhd->brnhqk', qs, kb,
                        preferred_element_type=jnp.float32) * (hd ** -0.5)
    qi = jnp.arange(BAND_BLK, dtype=jnp.int32)
    ki = jnp.arange(3 * BAND_BLK, dtype=jnp.int32) - BAND_BLK
    rel_steps = ki[None, :] - qi[:, None]
    key_m = jnp.arange(nblk, dtype=jnp.int32)[:, None] * BAND_BLK + ki[None, :]
    valid = ((jnp.abs(rel_steps) <= half)[None]
             & ((key_m >= 0) & (key_m < L))[:, None, :])
    bias = bias_tab[t5_bucket(rel_steps * dilation)].transpose(2, 0, 1)
    scores = scores + bias.astype(jnp.float32)
    scores = jnp.where(valid[None, None, :, None], scores, NEG_INF)
    lse = jax.nn.logsumexp(scores, axis=-1)
    p = jnp.exp(scores - lse[..., None])
    out = jnp.einsum('brnhqk,brnkhd->brnqhd', p.astype(v.dtype), vb)
    out = out.reshape(B, dilation, Lp, H, hd)[:, :, :L]
    out = out.transpose(0, 2, 1, 3, 4).reshape(B, S, H, hd)
    lse = lse.transpose(0, 1, 2, 4, 3).reshape(B, dilation, Lp, H)[:, :, :L]
    lse = lse.transpose(0, 2, 1, 3).reshape(B, S, H)
    return out, lse


def axial_rope_tables(rows):
    row = jnp.repeat(jnp.arange(rows, dtype=jnp.float32), GRID_W)
    col = jnp.tile(jnp.arange(GRID_W, dtype=jnp.float32), rows)
    n_freq = HEAD_DIM_B // 4
    freq = ROPE_THETA ** (-jnp.arange(n_freq, dtype=jnp.float32) / n_freq)
    ang = jnp.concatenate([row[:, None] * freq, col[:, None] * freq], axis=-1)
    return jnp.cos(ang), jnp.sin(ang)


def apply_rope(x, cos, sin):
    xf = x.astype(jnp.float32).reshape(*x.shape[:-1], x.shape[-1] // 2, 2)
    x0, x1 = xf[..., 0], xf[..., 1]
    c = cos[None, :, None, :]
    s = sin[None, :, None, :]
    out = jnp.stack([x0 * c - x1 * s, x0 * s + x1 * c], axis=-1)
    return out.reshape(x.shape).astype(x.dtype)


def gqa_axial_attention(q, k, v, q_norm, k_norm, cos, sin):
    B, S = q.shape[0], q.shape[1]
    q = apply_rope(rmsnorm(q, q_norm), cos, sin)
    k = apply_rope(rmsnorm(k, k_norm), cos, sin)
    scale = HEAD_DIM_B ** -0.5
    qblocks = q.reshape(B, S // Q_BLOCK, Q_BLOCK, N_KV_B, GQA_GROUP_B, HEAD_DIM_B).swapaxes(0, 1)

    def attn_block(qb):
        s = jnp.einsum('bqkgd,bskd->bkgqs', qb, k, preferred_element_type=jnp.float32) * scale
        p = jax.nn.softmax(s, axis=-1)
        return jnp.einsum('bkgqs,bskd->bqkgd', p.astype(v.dtype), v)

    ob = lax.map(attn_block, qblocks)
    return ob.swapaxes(0, 1).reshape(B, S, B_Q_WIDTH)


def hybrid_mixer(h, w_in, b_gate, q_norm, k_norm, rel_bias, w_branch_a, w_branch_b, w_out, cos, sin):
    B, S, D = h.shape
    proj = h @ w_in
    o1 = A_QKV_WIDTH
    o2 = o1 + B_Q_WIDTH
    o3 = o2 + B_KV_WIDTH
    o4 = o3 + B_KV_WIDTH
    pa, pq, pk, pv, pg = proj[..., :o1], proj[..., o1:o2], proj[..., o2:o3], proj[..., o3:o4], proj[..., o4:]

    a = pa.reshape(B, S, 3, N_GROUPS_A, HEADS_PER_GROUP_A, HEAD_DIM_A)
    bias_groups = rel_bias.reshape(N_BUCKETS, N_GROUPS_A, HEADS_PER_GROUP_A)
    outs, lses = [], []
    for g, (window, dil) in enumerate(DILATED_GROUPS):
        o, lse = dilated_group_attention(a[:, :, 0, g], a[:, :, 1, g], a[:, :, 2, g],
                                         bias_groups[:, g], dil, window // (2 * dil))
        outs.append(o)
        lses.append(lse)
    wgt = jax.nn.softmax(jnp.stack(lses, axis=0), axis=0)
    o_a = jnp.sum(wgt[..., None] * jnp.stack(outs, axis=0).astype(jnp.float32), axis=0)
    o_a = o_a.astype(h.dtype).reshape(B, S, GROUP_WIDTH_A)

    o_b = gqa_axial_attention(pq.reshape(B, S, N_HEADS_B, HEAD_DIM_B),
                              pk.reshape(B, S, N_KV_B, HEAD_DIM_B),
                              pv.reshape(B, S, N_KV_B, HEAD_DIM_B),
                              q_norm, k_norm, cos, sin)

    gates = jax.nn.sigmoid((pg + b_gate).reshape(B, S, N_BRANCHES, D))
    merged = gates[:, :, 0] * (o_a @ w_branch_a) + gates[:, :, 1] * (o_b @ w_branch_b)
    return merged @ w_out


def setup_inputs(seed: int = 0) -> dict:
    key = jax.random.key(seed)
    ks = jax.random.split(key, 24)
    f32 = jnp.float32

    def w(k, shape, fan_in):
        return jax.random.normal(k, shape, f32) * (fan_in ** -0.5)

    def gain(k, shape):
        return 1.0 + 0.05 * jax.random.normal(k, shape, f32)

    L, D = DEPTH, D_MODEL
    return {
        'x': jax.random.normal(ks[0], (BATCH, SEQ, D), f32),
        'ffn1_norm': gain(ks[1], (L, D)),
        'ffn1_w1': w(ks[2], (L, D, D_FF), D),
        'ffn1_w3': w(ks[3], (L, D, D_FF), D),
        'ffn1_w2': w(ks[4], (L, D_FF, D), D_FF),
        'mix_norm': gain(ks[5], (L, D)),
        'w_in': w(ks[6], (L, D, IN_WIDTH), D),
        'b_gate': 0.02 * jax.random.normal(ks[7], (L, GATE_WIDTH), f32),
        'q_norm': gain(ks[8], (L, HEAD_DIM_B)),
        'k_norm': gain(ks[9], (L, HEAD_DIM_B)),
        'rel_bias': 0.5 * jax.random.normal(ks[10], (N_BUCKETS, N_HEADS_A), f32),
        'w_branch_a': w(ks[11], (L, GROUP_WIDTH_A, D), GROUP_WIDTH_A),
        'w_branch_b': w(ks[12], (L, B_Q_WIDTH, D), B_Q_WIDTH),
        'w_out': w(ks[13], (L, D, D), D),
        'ffn2_norm': gain(ks[14], (L, D)),
        'ffn2_w1': w(ks[15], (L, D, D_FF), D),
        'ffn2_w3': w(ks[16], (L, D, D_FF), D),
        'ffn2_w2': w(ks[17], (L, D_FF, D), D_FF),
        'final_norm': gain(ks[18], (D,)),
    }


def reference(x, ffn1_norm, ffn1_w1, ffn1_w3, ffn1_w2, mix_norm, w_in, b_gate, q_norm, k_norm,
              rel_bias, w_branch_a, w_branch_b, w_out, ffn2_norm, ffn2_w1, ffn2_w3, ffn2_w2,
              final_norm):
    S = x.shape[1]
    rows = S // GRID_W
    cos, sin = axial_rope_tables(rows)
    for l in range(DEPTH):
        x = x + 0.5 * swiglu(rmsnorm(x, ffn1_norm[l]), ffn1_w1[l], ffn1_w3[l], ffn1_w2[l])
        h = rmsnorm(x, mix_norm[l])
        x = x + hybrid_mixer(h, w_in[l], b_gate[l], q_norm[l], k_norm[l], rel_bias,
                             w_branch_a[l], w_branch_b[l], w_out[l], cos, sin)
        x = x + 0.5 * swiglu(rmsnorm(x, ffn2_norm[l]), ffn2_w1[l], ffn2_w3[l], ffn2_w2[l])
    return rmsnorm(x, final_norm)
```

```python
import functools
import math

import jax
import jax.numpy as jnp
from jax import lax
from jax.experimental import pallas as pl
from jax.experimental.pallas import tpu as pltpu

D_MODEL = 1024
D_FF = 2816
DILATED_GROUPS = ((128, 1), (512, 4), (2048, 16))
N_GROUPS_A = 3
HEADS_A = 8
HD_A = 64
GW_A = HEADS_A * HD_A
N_HEADS_B = 8
N_KV_B = 2
GQA_B = N_HEADS_B // N_KV_B
HD_B = 128
GRID_W = 64
ROPE_THETA = 10000.0
N_BUCKETS = 32
MAX_DISTANCE = 1024
EPS = 1e-6
NEG_INF = -1e30
LOG2E = 1.4426950408889634

LANES = 128
TM = 512
FFN_CHUNK = 256
QBLK_A = 128
KWIN_A = 256
TQ_B = 256
KCH_B = 512
VMEM_LIMIT = 60 * 1024 * 1024

BF16 = jnp.bfloat16
F32 = jnp.float32


def _resident(shape):
    nd = len(shape)
    return pl.BlockSpec(shape, lambda *_: (0,) * nd, pipeline_mode=pl.Buffered(1))


def _rms(x, gain):
    ms = jnp.mean(x * x, axis=-1, keepdims=True)
    return x * lax.rsqrt(ms + EPS) * gain


def _ffn_kernel(x_ref, g_ref, w1_ref, w3_ref, w2_ref, fg_ref, o_ref, *, final_norm):
    x = x_ref[...]
    h = _rms(x, g_ref[...]).astype(BF16)
    acc = jnp.zeros((x.shape[0], D_MODEL), F32)
    for j in range(D_FF // FFN_CHUNK):
        c = slice(j * FFN_CHUNK, (j + 1) * FFN_CHUNK)
        a = jnp.dot(h, w1_ref[:, c], preferred_element_type=F32)
        b = jnp.dot(h, w3_ref[:, c], preferred_element_type=F32)
        g = (a * jax.nn.sigmoid(a) * b).astype(BF16)
        acc = acc + jnp.dot(g, w2_ref[c, :], preferred_element_type=F32)
    y = x + 0.5 * acc
    if final_norm:
        y = _rms(y, fg_ref[...])
    o_ref[...] = y


def _ffn(x2d, gain, w1, w3, w2, final_gain, final_norm):
    n = x2d.shape[0]
    row = pl.BlockSpec((TM, D_MODEL), lambda i: (i, 0))
    return pl.pallas_call(
        functools.partial(_ffn_kernel, final_norm=final_norm),
        grid=(n // TM,),
        in_specs=[row, _resident((1, D_MODEL)), _resident((D_MODEL, D_FF)), _resident((D_MODEL, D_FF)),
                  _resident((D_FF, D_MODEL)), _resident((1, D_MODEL))],
        out_specs=row,
        out_shape=jax.ShapeDtypeStruct((n, D_MODEL), F32),
        compiler_params=pltpu.CompilerParams(dimension_semantics=("arbitrary",), vmem_limit_bytes=VMEM_LIMIT),
        name="ffn_final" if final_norm else "ffn",
    )(x2d, gain, w1, w3, w2, final_gain)


A_WIDTH = 3 * GW_A
OFF_QB = N_GROUPS_A * A_WIDTH
OFF_KB = OFF_QB + N_HEADS_B * HD_B
OFF_VB = OFF_KB + N_KV_B * HD_B
OFF_GATE = OFF_VB + N_KV_B * HD_B
IN_WIDTH = OFF_GATE + 2 * D_MODEL


def _rope(y, cosf, sinf):
    return y * cosf + pltpu.roll(y, HD_B // 2, axis=1) * sinf


def _proj_kernel(x_ref, g_ref, w_ref, bg_ref, qn_ref, kn_ref, cos_ref, sin_ref,
                 a0_ref, a1_ref, a2_ref, qb_ref, kb_ref, vt_ref, gate_ref, slab_ref):
    h = _rms(x_ref[...], g_ref[...]).astype(BF16)
    tm = h.shape[0]

    def mm(lo, width):
        return jnp.dot(h, w_ref[:, lo:lo + width], preferred_element_type=F32)

    a0_ref[...] = mm(0, A_WIDTH).astype(BF16)
    for g, a_ref in ((1, a1_ref), (2, a2_ref)):
        d = DILATED_GROUPS[g][1]
        p = mm(g * A_WIDTH, A_WIDTH)
        for s in range(A_WIDTH // LANES):
            slab_ref[s] = p[:, s * LANES:(s + 1) * LANES]
        for r in range(d):
            for s in range(A_WIDTH // LANES):
                a_ref[r, :, s * LANES:(s + 1) * LANES] = (
                    slab_ref[s, pl.ds(r, tm // d, stride=d), :].astype(BF16))

    cosf = cos_ref[...]
    sinf = sin_ref[...]
    pq = mm(OFF_QB, N_HEADS_B * HD_B)
    for hh in range(N_HEADS_B):
        c = slice(hh * HD_B, (hh + 1) * HD_B)
        qb_ref[:, c] = _rope(_rms(pq[:, c], qn_ref[...]), cosf, sinf).astype(BF16)
    pkv = mm(OFF_KB, 2 * N_KV_B * HD_B)
    for hh in range(N_KV_B):
        c = slice(hh * HD_B, (hh + 1) * HD_B)
        kb_ref[:, c] = _rope(_rms(pkv[:, c], kn_ref[...]), cosf, sinf).astype(BF16)
    vt_ref[...] = pkv[:, N_KV_B * HD_B:].T.astype(BF16)
    pg = mm(OFF_GATE, 2 * D_MODEL)
    gate_ref[...] = jax.nn.sigmoid(pg + bg_ref[...]).astype(BF16)


def _proj(x1, gain, w_in, b_gate, qn, kn, cosf, sinf):
    bsz, seq, _ = x1.shape
    nt = seq // TM
    outs, specs = [], []
    for _, d in DILATED_GROUPS:
        outs.append(jax.ShapeDtypeStruct((bsz, d, seq // d, A_WIDTH), BF16))
        specs.append(pl.BlockSpec((None, d, TM // d, A_WIDTH), lambda b, t: (b, 0, t, 0)))
    outs += [jax.ShapeDtypeStruct((bsz, seq, N_HEADS_B * HD_B), BF16),
             jax.ShapeDtypeStruct((bsz, seq, N_KV_B * HD_B), BF16),
             jax.ShapeDtypeStruct((bsz, N_KV_B * HD_B, seq), BF16),
             jax.ShapeDtypeStruct((bsz, seq, 2 * D_MODEL), BF16)]
    specs += [pl.BlockSpec((None, TM, N_HEADS_B * HD_B), lambda b, t: (b, t, 0)),
              pl.BlockSpec((None, TM, N_KV_B * HD_B), lambda b, t: (b, t, 0)),
              pl.BlockSpec((None, N_KV_B * HD_B, TM), lambda b, t: (b, 0, t)),
              pl.BlockSpec((None, TM, 2 * D_MODEL), lambda b, t: (b, t, 0))]
    specs[0] = pl.BlockSpec((None, None, TM, A_WIDTH), lambda b, t: (b, 0, t, 0))
    return pl.pallas_call(
        _proj_kernel,
        grid=(bsz, nt),
        in_specs=[pl.BlockSpec((None, TM, D_MODEL), lambda b, t: (b, t, 0)),
                  _resident((1, D_MODEL)), _resident((D_MODEL, IN_WIDTH)), _resident((1, 2 * D_MODEL)),
                  _resident((1, HD_B)), _resident((1, HD_B)),
                  pl.BlockSpec((TM, HD_B), lambda b, t: (t, 0)),
                  pl.BlockSpec((TM, HD_B), lambda b, t: (t, 0))],
        out_specs=specs,
        out_shape=outs,
        scratch_shapes=[pltpu.VMEM((A_WIDTH // LANES, TM, LANES), F32)],
        compiler_params=pltpu.CompilerParams(dimension_semantics=("arbitrary", "arbitrary"),
                                             vmem_limit_bytes=VMEM_LIMIT),
        name="proj",
    )(x1, gain, w_in, b_gate, qn, kn, cosf, sinf)


def _dilated_kernel(bucket_ref, tab_ref, q_ref, k_ref, v_ref, o_ref, lse_ref, bias_ref, *, sub_len):
    nblk = sub_len // QBLK_A

    @pl.when((pl.program_id(0) == 0) & (pl.program_id(1) == 0))
    def _build_bias():
        for v in range(3):
            bt = bucket_ref[v]
            for hh in range(HEADS_A):
                acc = jnp.full((QBLK_A, KWIN_A), NEG_INF, F32)
                for j in range(N_BUCKETS):
                    acc = jnp.where(bt == j, tab_ref[j, hh], acc)
                bias_ref[v, hh] = acc

    lane = lax.broadcasted_iota(jnp.int32, (QBLK_A, LANES), 1)
    low = lane < HD_A

    def body(n, carry):
        q0 = pl.multiple_of(n * QBLK_A, QBLK_A)
        w0 = pl.multiple_of(jnp.clip(n * QBLK_A - (KWIN_A - QBLK_A) // 2, 0, sub_len - KWIN_A), 64)
        variant = jnp.where(n == 0, 0, jnp.where(n == nblk - 1, 2, 1))
        for pair in range(HEADS_A // 2):
            c = slice(pair * LANES, (pair + 1) * LANES)
            qp = q_ref[pl.ds(q0, QBLK_A), c]
            kp = k_ref[pl.ds(w0, KWIN_A), c]
            vp = v_ref[pl.ds(w0, KWIN_A), c]
            outs, lses = [], []
            for half in range(2):
                keep = low if half == 0 else jnp.logical_not(low)
                qh = jnp.where(keep, qp, jnp.zeros_like(qp))
                s = lax.dot_general(qh, kp, (((1,), (1,)), ((), ())), preferred_element_type=F32)
                s = s + bias_ref[variant, 2 * pair + half]
                m = jnp.max(s, axis=-1, keepdims=True)
                p = jnp.exp(s - m)
                l = jnp.sum(p, axis=-1, keepdims=True)
                o = jnp.dot(p.astype(BF16), vp, preferred_element_type=F32)
                outs.append(o / l)
                lses.append(jnp.broadcast_to(m + jnp.log(l), (QBLK_A, LANES)))
            o_ref[pl.ds(q0, QBLK_A), c] = jnp.where(low, outs[0], outs[1]).astype(BF16)
            lse_ref[pl.ds(q0, QBLK_A), c] = jnp.where(low, lses[0], lses[1])
        return carry

    lax.fori_loop(0, nblk, body, 0)


def _dilated(a_g, buckets, table, g):
    bsz, d, sub_len, _ = a_g.shape
    assert sub_len % QBLK_A == 0 and sub_len >= KWIN_A
    blk = lambda col: pl.BlockSpec((None, None, sub_len, GW_A), lambda b, r: (b, r, 0, col))
    return pl.pallas_call(
        functools.partial(_dilated_kernel, sub_len=sub_len),
        grid=(bsz, d),
        in_specs=[_resident((3, QBLK_A, KWIN_A)),
                  pl.BlockSpec(memory_space=pltpu.SMEM),
                  blk(0), blk(1), blk(2)],
        out_specs=[blk(0), blk(0)],
        out_shape=[jax.ShapeDtypeStruct((bsz, d, sub_len, GW_A), BF16),
                   jax.ShapeDtypeStruct((bsz, d, sub_len, GW_A), F32)],
        scratch_shapes=[pltpu.VMEM((3, HEADS_A, QBLK_A, KWIN_A), F32)],
        compiler_params=pltpu.CompilerParams(dimension_semantics=("arbitrary", "arbitrary"),
                                             vmem_limit_bytes=VMEM_LIMIT),
        name=f"dilated{g}",
    )(buckets, table, a_g, a_g, a_g)


def _gqa_kernel(q_ref, k_ref, vt_ref, o_ref, s_ref, p_ref):
    seq = k_ref.shape[0]
    nch = seq // KCH_B

    def body(qt, carry):
        q0 = pl.multiple_of(qt * TQ_B, TQ_B)
        for g in range(GQA_B):
            c = slice(g * HD_B, (g + 1) * HD_B)
            q = q_ref[pl.ds(q0, TQ_B), c]
            mrun = jnp.full((8, TQ_B), -jnp.inf, F32)
            for ch in range(nch):
                rows = slice(ch * KCH_B, (ch + 1) * KCH_B)
                st = lax.dot_general(k_ref[rows, :], q, (((1,), (1,)), ((), ())),
                                     preferred_element_type=F32)
                s_ref[g, rows, :] = st
                mrun = jnp.maximum(mrun, jnp.max(st.reshape(KCH_B // 8, 8, TQ_B), axis=0))
            m = jnp.max(mrun, axis=0, keepdims=True)
            lrun = jnp.zeros((8, TQ_B), F32)
            for ch in range(nch):
                rows = slice(ch * KCH_B, (ch + 1) * KCH_B)
                p = jnp.exp2(s_ref[g, rows, :] - m)
                lrun = lrun + jnp.sum(p.reshape(KCH_B // 8, 8, TQ_B), axis=0)
                p_ref[g, rows, :] = p.astype(BF16)
            l = jnp.sum(lrun, axis=0, keepdims=True)
            ot = jnp.dot(vt_ref[...], p_ref[g], preferred_element_type=F32)
            o_ref[pl.ds(q0, TQ_B), c] = (ot / l).T.astype(BF16)
        return carry

    lax.fori_loop(0, seq // TQ_B, body, 0)


def _gqa(qb, kb, vt):
    bsz, seq, _ = qb.shape
    gw = GQA_B * HD_B
    return pl.pallas_call(
        _gqa_kernel,
        grid=(bsz, N_KV_B),
        in_specs=[pl.BlockSpec((None, seq, gw), lambda b, kv: (b, 0, kv)),
                  pl.BlockSpec((None, seq, HD_B), lambda b, kv: (b, 0, kv)),
                  pl.BlockSpec((None, HD_B, seq), lambda b, kv: (b, kv, 0))],
        out_specs=pl.BlockSpec((None, seq, gw), lambda b, kv: (b, 0, kv)),
        out_shape=jax.ShapeDtypeStruct((bsz, seq, N_HEADS_B * HD_B), BF16),
        scratch_shapes=[pltpu.VMEM((GQA_B, seq, TQ_B), F32), pltpu.VMEM((GQA_B, seq, TQ_B), BF16)],
        compiler_params=pltpu.CompilerParams(dimension_semantics=("arbitrary", "arbitrary"),
                                             vmem_limit_bytes=VMEM_LIMIT),
        name="gqa",
    )(qb, kb, vt)


def _merge_kernel(x_ref, gate_ref, ob_ref, o0_ref, l0_ref, o1_ref, l1_ref, o2_ref, l2_ref,
                  wa_ref, wb_ref, wo_ref, out_ref, oslab_ref, lslab_ref):
    tm = x_ref.shape[0]
    nslab = GW_A // LANES

    for g, (o_ref, l_ref) in enumerate(((o1_ref, l1_ref), (o2_ref, l2_ref))):
        d = DILATED_GROUPS[g + 1][1]
        for r in range(d):
            for s in range(nslab):
                c = slice(s * LANES, (s + 1) * LANES)
                oslab_ref[g, s, pl.ds(r, tm // d, stride=d), :] = o_ref[r, :, c].astype(F32)
                lslab_ref[g, s, pl.ds(r, tm // d, stride=d), :] = l_ref[r, :, c]

    parts = []
    for s in range(nslab):
        c = slice(s * LANES, (s + 1) * LANES)
        ls = [l0_ref[:, c], lslab_ref[0, s], lslab_ref[1, s]]
        os_ = [o0_ref[:, c].astype(F32), oslab_ref[0, s], oslab_ref[1, s]]
        m = jnp.maximum(jnp.maximum(ls[0], ls[1]), ls[2])
        es = [jnp.exp(l - m) for l in ls]
        den = es[0] + es[1] + es[2]
        num = es[0] * os_[0] + es[1] * os_[1] + es[2] * os_[2]
        parts.append((num / den).astype(BF16))
    o_a = jnp.concatenate(parts, axis=-1)

    ya = jnp.dot(o_a, wa_ref[...], preferred_element_type=F32)
    yb = jnp.dot(ob_ref[...], wb_ref[...], preferred_element_type=F32)
    merged = (gate_ref[:, :D_MODEL].astype(F32) * ya + gate_ref[:, D_MODEL:].astype(F32) * yb).astype(BF16)
    out_ref[...] = x_ref[...] + jnp.dot(merged, wo_ref[...], preferred_element_type=F32)


def _merge(x1, gates, ob, outs, lses, wa, wb, wo):
    bsz, seq, _ = x1.shape
    tok = lambda w: pl.BlockSpec((None, TM, w), lambda b, t: (b, t, 0))
    a_specs = []
    for g, (_, d) in enumerate(DILATED_GROUPS):
        if d == 1:
            spec = pl.BlockSpec((None, None, TM, GW_A), lambda b, t: (b, 0, t, 0))
        else:
            spec = pl.BlockSpec((None, d, TM // d, GW_A), lambda b, t: (b, 0, t, 0))
        a_specs += [spec, spec]
    a_args = [arr for pair in zip(outs, lses) for arr in pair]
    return pl.pallas_call(
        _merge_kernel,
        grid=(bsz, seq // TM),
        in_specs=[tok(D_MODEL), tok(2 * D_MODEL), tok(N_HEADS_B * HD_B)] + a_specs
                 + [_resident((GW_A, D_MODEL)), _resident((N_HEADS_B * HD_B, D_MODEL)),
                    _resident((D_MODEL, D_MODEL))],
        out_specs=tok(D_MODEL),
        out_shape=jax.ShapeDtypeStruct((bsz, seq, D_MODEL), F32),
        scratch_shapes=[pltpu.VMEM((2, GW_A // LANES, TM, LANES), F32),
                        pltpu.VMEM((2, GW_A // LANES, TM, LANES), F32)],
        compiler_params=pltpu.CompilerParams(dimension_semantics=("arbitrary", "arbitrary"),
                                             vmem_limit_bytes=VMEM_LIMIT),
        name="merge",
    )(x1, gates, ob, *a_args, wa, wb, wo)


def _t5_bucket(rel):
    n = N_BUCKETS // 2
    max_exact = n // 2
    ret = jnp.where(rel > 0, n, 0)
    a = jnp.abs(rel)
    af = jnp.maximum(a, 1).astype(F32)
    large = max_exact + (jnp.log(af / max_exact) / math.log(MAX_DISTANCE / max_exact)
                         * (n - max_exact)).astype(jnp.int32)
    large = jnp.minimum(large, n - 1)
    return ret + jnp.where(a < max_exact, a, large)


def _bucket_tiles(dilation, half):
    qi = jnp.arange(QBLK_A, dtype=jnp.int32)[:, None]
    kj = jnp.arange(KWIN_A, dtype=jnp.int32)[None, :]
    tiles = []
    for shift in (0, (KWIN_A - QBLK_A) // 2, KWIN_A - QBLK_A):
        rel = kj - shift - qi
        tiles.append(jnp.where(jnp.abs(rel) <= half, _t5_bucket(rel * dilation), -1))
    return jnp.stack(tiles)


def _w_in_layout():
    perm = list(range(0, HD_B, 2)) + list(range(1, HD_B, 2))
    cols, scale = [], []
    for g in range(N_GROUPS_A):
        for part in range(3):
            lo = (part * N_GROUPS_A + g) * GW_A
            cols += list(range(lo, lo + GW_A))
            scale += [HD_A ** -0.5 if part == 0 else 1.0] * GW_A
    for hh in range(N_HEADS_B + N_KV_B):
        cols += [OFF_QB + hh * HD_B + p for p in perm]
    cols += list(range(OFF_VB, IN_WIDTH))
    scale += [1.0] * (IN_WIDTH - OFF_QB)
    return perm, cols, scale


def kernel(x, ffn1_norm, ffn1_w1, ffn1_w3, ffn1_w2, mix_norm, w_in, b_gate, q_norm, k_norm, rel_bias,
           w_branch_a, w_branch_b, w_out, ffn2_norm, ffn2_w1, ffn2_w3, ffn2_w2, final_norm):
    bsz, seq, dm = x.shape
    assert dm == D_MODEL and seq % TM == 0 and ffn1_w1.shape[0] == 1
    row = lambda v: v.reshape(1, -1).astype(F32)
    perm, cols, scale = _w_in_layout()
    perm = jnp.asarray(perm, jnp.int32)

    t = jnp.arange(seq, dtype=jnp.int32)
    n_freq = HD_B // 4
    freq = ROPE_THETA ** (-jnp.arange(n_freq, dtype=F32) / n_freq)
    ang = jnp.concatenate([(t // GRID_W).astype(F32)[:, None] * freq, (t % GRID_W).astype(F32)[:, None] * freq], -1)
    cosf = jnp.concatenate([jnp.cos(ang), jnp.cos(ang)], -1)
    sinf = jnp.concatenate([-jnp.sin(ang), jnp.sin(ang)], -1)

    w_in_k = (w_in[0][:, jnp.asarray(cols, jnp.int32)] * jnp.asarray(scale, F32)).astype(BF16)
    qn = row(q_norm[0][perm] * (HD_B ** -0.5 * LOG2E))
    kn = row(k_norm[0][perm])

    x1 = _ffn(x.reshape(bsz * seq, dm), row(ffn1_norm[0]), ffn1_w1[0].astype(BF16), ffn1_w3[0].astype(BF16),
              ffn1_w2[0].astype(BF16), row(final_norm), False).reshape(bsz, seq, dm)
    a0, a1, a2, qb, kb, vt, gates = _proj(x1, row(mix_norm[0]), w_in_k, row(b_gate[0]), qn, kn, cosf, sinf)

    outs, lses = [], []
    for g, (a_g, (window, d)) in enumerate(zip((a0, a1, a2), DILATED_GROUPS)):
        table = rel_bias[:, g * HEADS_A:(g + 1) * HEADS_A].astype(F32)
        o_g, lse_g = _dilated(a_g, _bucket_tiles(d, window // (2 * d)), table, g)
        outs.append(o_g)
        lses.append(lse_g)
    ob = _gqa(qb, kb, vt)

    x2 = _merge(x1, gates, ob, outs, lses, w_branch_a[0].astype(BF16), w_branch_b[0].astype(BF16),
                w_out[0].astype(BF16))
    y = _ffn(x2.reshape(bsz * seq, dm), row(ffn2_norm[0]), ffn2_w1[0].astype(BF16), ffn2_w3[0].astype(BF16),
             ffn2_w2[0].astype(BF16), row(final_norm), True)
    return y.reshape(bsz, seq, dm)
```

```python
import functools
import math

import jax
import jax.numpy as jnp
from jax import lax
from jax.experimental import pallas as pl
from jax.experimental.pallas import tpu as pltpu

D_MODEL = 1024
D_FF = 2816
DILATED_GROUPS = ((128, 1), (512, 4), (2048, 16))
N_GROUPS_A = 3
HEADS_A = 8
HD_A = 64
GW_A = HEADS_A * HD_A
N_HEADS_B = 8
N_KV_B = 2
GQA_B = N_HEADS_B // N_KV_B
HD_B = 128
GRID_W = 64
ROPE_THETA = 10000.0
N_BUCKETS = 32
MAX_DISTANCE = 1024
EPS = 1e-6
NEG_INF = -1e30
LOG2E = 1.4426950408889634

LANES = 128
TM = 512
FFN_CHUNK = 256
QBLK_A = 128
KWIN_A = 256
TQ_B = 256
KCH_B = 512
VMEM_LIMIT = 60 * 1024 * 1024

BF16 = jnp.bfloat16
F32 = jnp.float32


def _resident(shape):
    nd = len(shape)
    return pl.BlockSpec(shape, lambda *_: (0,) * nd, pipeline_mode=pl.Buffered(1))


def _rms(x, gain):
    ms = jnp.mean(x * x, axis=-1, keepdims=True)
    return x * lax.rsqrt(ms + EPS) * gain


def _ffn_kernel(x_ref, g_ref, w1_ref, w3_ref, w2_ref, fg_ref, o_ref, *, final_norm):
    x = x_ref[...]
    h = _rms(x, g_ref[...]).astype(BF16)
    acc = jnp.zeros((x.shape[0], D_MODEL), F32)
    for j in range(D_FF // FFN_CHUNK):
        c = slice(j * FFN_CHUNK, (j + 1) * FFN_CHUNK)
        a = jnp.dot(h, w1_ref[:, c], preferred_element_type=F32)
        b = jnp.dot(h, w3_ref[:, c], preferred_element_type=F32)
        g = (a * jax.nn.sigmoid(a) * b).astype(BF16)
        acc = acc + jnp.dot(g, w2_ref[c, :], preferred_element_type=F32)
    y = x + 0.5 * acc
    if final_norm:
        y = _rms(y, fg_ref[...])
    o_ref[...] = y


def _ffn(x2d, gain, w1, w3, w2, final_gain, final_norm):
    n = x2d.shape[0]
    row = pl.BlockSpec((TM, D_MODEL), lambda i: (i, 0))
    return pl.pallas_call(
        functools.partial(_ffn_kernel, final_norm=final_norm),
        grid=(n // TM,),
        in_specs=[row, _resident((1, D_MODEL)), _resident((D_MODEL, D_FF)), _resident((D_MODEL, D_FF)),
                  _resident((D_FF, D_MODEL)), _resident((1, D_MODEL))],
        out_specs=row,
        out_shape=jax.ShapeDtypeStruct((n, D_MODEL), F32),
        compiler_params=pltpu.CompilerParams(dimension_semantics=("arbitrary",), vmem_limit_bytes=VMEM_LIMIT),
        name="ffn_final" if final_norm else "ffn",
    )(x2d, gain, w1, w3, w2, final_gain)


A_WIDTH = 3 * GW_A
OFF_QB = N_GROUPS_A * A_WIDTH
OFF_KB = OFF_QB + N_HEADS_B * HD_B
OFF_VB = OFF_KB + N_KV_B * HD_B
OFF_GATE = OFF_VB + N_KV_B * HD_B
IN_WIDTH = OFF_GATE + 2 * D_MODEL


def _rope(y, cosf, sinf):
    return y * cosf + pltpu.roll(y, HD_B // 2, axis=1) * sinf


def _proj_kernel(x_ref, g_ref, w_ref, bg_ref, qn_ref, kn_ref, cos_ref, sin_ref,
                 a0_ref, a1_ref, a2_ref, qb_ref, kb_ref, vt_ref, gate_ref, slab_ref):
    h = _rms(x_ref[...], g_ref[...]).astype(BF16)
    tm = h.shape[0]

    def mm(lo, width):
        return jnp.dot(h, w_ref[:, lo:lo + width], preferred_element_type=F32)

    a0_ref[...] = mm(0, A_WIDTH).astype(BF16)
    for g, a_ref in ((1, a1_ref), (2, a2_ref)):
        d = DILATED_GROUPS[g][1]
        p = mm(g * A_WIDTH, A_WIDTH)
        for s in range(A_WIDTH // LANES):
            slab_ref[s] = p[:, s * LANES:(s + 1) * LANES]
        for r in range(d):
            for s in range(A_WIDTH // LANES):
                a_ref[r, :, s * LANES:(s + 1) * LANES] = (
                    slab_ref[s, pl.ds(r, tm // d, stride=d), :].astype(BF16))

    cosf = cos_ref[...]
    sinf = sin_ref[...]
    pq = mm(OFF_QB, N_HEADS_B * HD_B)
    for hh in range(N_HEADS_B):
        c = slice(hh * HD_B, (hh + 1) * HD_B)
        qb_ref[:, c] = _rope(_rms(pq[:, c], qn_ref[...]), cosf, sinf).astype(BF16)
    pkv = mm(OFF_KB, 2 * N_KV_B * HD_B)
    for hh in range(N_KV_B):
        c = slice(hh * HD_B, (hh + 1) * HD_B)
        kb_ref[:, c] = _rope(_rms(pkv[:, c], kn_ref[...]), cosf, sinf).astype(BF16)
    vt_ref[...] = pkv[:, N_KV_B * HD_B:].T.astype(BF16)
    pg = mm(OFF_GATE, 2 * D_MODEL)
    gate_ref[...] = jax.nn.sigmoid(pg + bg_ref[...]).astype(BF16)


def _proj(x1, gain, w_in, b_gate, qn, kn, cosf, sinf):
    bsz, seq, _ = x1.shape
    nt = seq // TM
    outs, specs = [], []
    for _, d in DILATED_GROUPS:
        outs.append(jax.ShapeDtypeStruct((bsz, d, seq // d, A_WIDTH), BF16))
        specs.append(pl.BlockSpec((None, d, TM // d, A_WIDTH), lambda b, t: (b, 0, t, 0)))
    outs += [jax.ShapeDtypeStruct((bsz, seq, N_HEADS_B * HD_B), BF16),
             jax.ShapeDtypeStruct((bsz, seq, N_KV_B * HD_B), BF16),
             jax.ShapeDtypeStruct((bsz, N_KV_B * HD_B, seq), BF16),
             jax.ShapeDtypeStruct((bsz, seq, 2 * D_MODEL), BF16)]
    specs += [pl.BlockSpec((None, TM, N_HEADS_B * HD_B), lambda b, t: (b, t, 0)),
              pl.BlockSpec((None, TM, N_KV_B * HD_B), lambda b, t: (b, t, 0)),
              pl.BlockSpec((None, N_KV_B * HD_B, TM), lambda b, t: (b, 0, t)),
              pl.BlockSpec((None, TM, 2 * D_MODEL), lambda b, t: (b, t, 0))]
    specs[0] = pl.BlockSpec((None, None, TM, A_WIDTH), lambda b, t: (b, 0, t, 0))
    return pl.pallas_call(
        _proj_kernel,
        grid=(bsz, nt),
        in_specs=[pl.BlockSpec((None, TM, D_MODEL), lambda b, t: (b, t, 0)),
                  _resident((1, D_MODEL)), _resident((D_MODEL, IN_WIDTH)), _resident((1, 2 * D_MODEL)),
                  _resident((1, HD_B)), _resident((1, HD_B)),
                  pl.BlockSpec((TM, HD_B), lambda b, t: (t, 0)),
                  pl.BlockSpec((TM, HD_B), lambda b, t: (t, 0))],
        out_specs=specs,
        out_shape=outs,
        scratch_shapes=[pltpu.VMEM((A_WIDTH // LANES, TM, LANES), F32)],
        compiler_params=pltpu.CompilerParams(dimension_semantics=("arbitrary", "arbitrary"),
                                             vmem_limit_bytes=VMEM_LIMIT),
        name="proj",
    )(x1, gain, w_in, b_gate, qn, kn, cosf, sinf)


def _dilated_kernel(bucket_ref, tab_ref, q_ref, k_ref, v_ref, o_ref, lse_ref, bias_ref, s_ref, *, sub_len):
    nblk = sub_len // QBLK_A
    npair = HEADS_A // 2

    @pl.when((pl.program_id(0) == 0) & (pl.program_id(1) == 0))
    def _build_bias():
        for v in range(3):
            bt = bucket_ref[v]
            for hh in range(HEADS_A):
                acc = jnp.full((QBLK_A, KWIN_A), NEG_INF, F32)
                for j in range(N_BUCKETS):
                    acc = jnp.where(bt == j, tab_ref[j, hh] * LOG2E, acc)
                bias_ref[v, hh // 2, (hh % 2) * QBLK_A:(hh % 2 + 1) * QBLK_A, :] = acc

    lane = lax.broadcasted_iota(jnp.int32, (QBLK_A, LANES), 1)
    low = lane < HD_A

    def geometry(n):
        q0 = pl.multiple_of(n * QBLK_A, QBLK_A)
        w0 = pl.multiple_of(jnp.clip(n * QBLK_A - (KWIN_A - QBLK_A) // 2, 0, sub_len - KWIN_A), 64)
        variant = jnp.where(n == 0, 0, jnp.where(n == nblk - 1, 2, 1))
        return q0, w0, variant

    def score_pass(n, pair):
        q0, w0, variant = geometry(n)
        c = slice(pair * LANES, (pair + 1) * LANES)
        qp = q_ref[pl.ds(q0, QBLK_A), c]
        kp = k_ref[pl.ds(w0, KWIN_A), c]
        zero = jnp.zeros_like(qp)
        q2 = jnp.concatenate([jnp.where(low, qp, zero), jnp.where(low, zero, qp)], axis=0)
        s = lax.dot_general(q2, kp, (((1,), (1,)), ((), ())), preferred_element_type=F32)
        s = s + bias_ref[variant, pair]
        s_ref[pair] = s
        return jnp.max(s, axis=-1, keepdims=True)

    def value_pass(n, pair, m):
        q0, w0, _ = geometry(n)
        c = slice(pair * LANES, (pair + 1) * LANES)
        vp = v_ref[pl.ds(w0, KWIN_A), c]
        p = jnp.exp2(s_ref[pair] - m)
        l = jnp.sum(p, axis=-1, keepdims=True)
        o = jnp.dot(p.astype(BF16), vp, preferred_element_type=F32) / l
        lse = jnp.broadcast_to(m + jnp.log2(l), (2 * QBLK_A, LANES))
        o_ref[pl.ds(q0, QBLK_A), c] = jnp.where(low, o[:QBLK_A], o[QBLK_A:]).astype(BF16)
        lse_ref[pl.ds(q0, QBLK_A), c] = jnp.where(low, lse[:QBLK_A], lse[QBLK_A:])

    def body(n, ms):
        for pair in range(npair):
            if pair + 1 < npair:
                nxt = score_pass(n, pair + 1)
            else:
                nxt = score_pass(jnp.minimum(n + 1, nblk - 1), 0)
            value_pass(n, pair, ms)
            ms = nxt
        return ms

    lax.fori_loop(0, nblk, body, score_pass(0, 0))


def _dilated(a_g, buckets, table, g):
    bsz, d, sub_len, _ = a_g.shape
    assert sub_len % QBLK_A == 0 and sub_len >= KWIN_A
    blk = lambda col: pl.BlockSpec((None, None, sub_len, GW_A), lambda b, r: (b, r, 0, col))
    return pl.pallas_call(
        functools.partial(_dilated_kernel, sub_len=sub_len),
        grid=(bsz, d),
        in_specs=[_resident((3, QBLK_A, KWIN_A)),
                  pl.BlockSpec(memory_space=pltpu.SMEM),
                  blk(0), blk(1), blk(2)],
        out_specs=[blk(0), blk(0)],
        out_shape=[jax.ShapeDtypeStruct((bsz, d, sub_len, GW_A), BF16),
                   jax.ShapeDtypeStruct((bsz, d, sub_len, GW_A), F32)],
        scratch_shapes=[pltpu.VMEM((3, HEADS_A // 2, 2 * QBLK_A, KWIN_A), F32),
                        pltpu.VMEM((HEADS_A // 2, 2 * QBLK_A, KWIN_A), F32)],
        compiler_params=pltpu.CompilerParams(dimension_semantics=("arbitrary", "arbitrary"),
                                             vmem_limit_bytes=VMEM_LIMIT),
        name=f"dilated{g}",
    )(buckets, table, a_g, a_g, a_g)


def _gqa_kernel(q_ref, k_ref, vt_ref, o_ref, s_ref, p_ref):
    seq = k_ref.shape[0]
    nch = seq // KCH_B
    nqt = seq // TQ_B

    def score_pass(qt, g):
        q0 = pl.multiple_of(qt * TQ_B, TQ_B)
        q = q_ref[pl.ds(q0, TQ_B), g * HD_B:(g + 1) * HD_B]
        mrun = jnp.full((8, TQ_B), NEG_INF, F32)
        for ch in range(nch):
            rows = slice(ch * KCH_B, (ch + 1) * KCH_B)
            st = lax.dot_general(k_ref[rows, :], q, (((1,), (1,)), ((), ())),
                                 preferred_element_type=F32)
            s_ref[g, rows, :] = st
            mrun = jnp.maximum(mrun, jnp.max(st.reshape(KCH_B // 8, 8, TQ_B), axis=0))
        return mrun

    def value_pass(qt, g, mrun):
        q0 = pl.multiple_of(qt * TQ_B, TQ_B)
        m = jnp.max(mrun, axis=0, keepdims=True)
        lrun = jnp.zeros((8, TQ_B), F32)
        for ch in range(nch):
            rows = slice(ch * KCH_B, (ch + 1) * KCH_B)
            p = jnp.exp2(s_ref[g, rows, :] - m)
            lrun = lrun + jnp.sum(p.reshape(KCH_B // 8, 8, TQ_B), axis=0)
            p_ref[g, rows, :] = p.astype(BF16)
        l = jnp.sum(lrun, axis=0, keepdims=True)
        ot = jnp.dot(vt_ref[...], p_ref[g], preferred_element_type=F32)
        o_ref[pl.ds(q0, TQ_B), g * HD_B:(g + 1) * HD_B] = (ot / l).T.astype(BF16)

    def body(qt, mrun):
        for g in range(GQA_B):
            if g + 1 < GQA_B:
                nxt = score_pass(qt, g + 1)
            else:
                nxt = score_pass(jnp.minimum(qt + 1, nqt - 1), 0)
            value_pass(qt, g, mrun)
            mrun = nxt
        return mrun

    lax.fori_loop(0, nqt, body, score_pass(0, 0))


def _gqa(qb, kb, vt):
    bsz, seq, _ = qb.shape
    gw = GQA_B * HD_B
    return pl.pallas_call(
        _gqa_kernel,
        grid=(bsz, N_KV_B),
        in_specs=[pl.BlockSpec((None, seq, gw), lambda b, kv: (b, 0, kv)),
                  pl.BlockSpec((None, seq, HD_B), lambda b, kv: (b, 0, kv)),
                  pl.BlockSpec((None, HD_B, seq), lambda b, kv: (b, kv, 0))],
        out_specs=pl.BlockSpec((None, seq, gw), lambda b, kv: (b, 0, kv)),
        out_shape=jax.ShapeDtypeStruct((bsz, seq, N_HEADS_B * HD_B), BF16),
        scratch_shapes=[pltpu.VMEM((GQA_B, seq, TQ_B), F32), pltpu.VMEM((GQA_B, seq, TQ_B), BF16)],
        compiler_params=pltpu.CompilerParams(dimension_semantics=("arbitrary", "arbitrary"),
                                             vmem_limit_bytes=VMEM_LIMIT),
        name="gqa",
    )(qb, kb, vt)


def _merge_kernel(x_ref, gate_ref, ob_ref, o0_ref, l0_ref, o1_ref, l1_ref, o2_ref, l2_ref,
                  wa_ref, wb_ref, wo_ref, out_ref, oslab_ref, lslab_ref):
    tm = x_ref.shape[0]
    nslab = GW_A // LANES

    for g, (o_ref, l_ref) in enumerate(((o1_ref, l1_ref), (o2_ref, l2_ref))):
        d = DILATED_GROUPS[g + 1][1]
        for r in range(d):
            for s in range(nslab):
                c = slice(s * LANES, (s + 1) * LANES)
                oslab_ref[g, s, pl.ds(r, tm // d, stride=d), :] = o_ref[r, :, c].astype(F32)
                lslab_ref[g, s, pl.ds(r, tm // d, stride=d), :] = l_ref[r, :, c]

    parts = []
    for s in range(nslab):
        c = slice(s * LANES, (s + 1) * LANES)
        ls = [l0_ref[:, c], lslab_ref[0, s], lslab_ref[1, s]]
        os_ = [o0_ref[:, c].astype(F32), oslab_ref[0, s], oslab_ref[1, s]]
        m = jnp.maximum(jnp.maximum(ls[0], ls[1]), ls[2])
        es = [jnp.exp2(l - m) for l in ls]
        den = es[0] + es[1] + es[2]
        num = es[0] * os_[0] + es[1] * os_[1] + es[2] * os_[2]
        parts.append((num / den).astype(BF16))
    o_a = jnp.concatenate(parts, axis=-1)

    ya = jnp.dot(o_a, wa_ref[...], preferred_element_type=F32)
    yb = jnp.dot(ob_ref[...], wb_ref[...], preferred_element_type=F32)
    merged = (gate_ref[:, :D_MODEL].astype(F32) * ya + gate_ref[:, D_MODEL:].astype(F32) * yb).astype(BF16)
    out_ref[...] = x_ref[...] + jnp.dot(merged, wo_ref[...], preferred_element_type=F32)


def _merge(x1, gates, ob, outs, lses, wa, wb, wo):
    bsz, seq, _ = x1.shape
    tok = lambda w: pl.BlockSpec((None, TM, w), lambda b, t: (b, t, 0))
    a_specs = []
    for g, (_, d) in enumerate(DILATED_GROUPS):
        if d == 1:
            spec = pl.BlockSpec((None, None, TM, GW_A), lambda b, t: (b, 0, t, 0))
        else:
            spec = pl.BlockSpec((None, d, TM // d, GW_A), lambda b, t: (b, 0, t, 0))
        a_specs += [spec, spec]
    a_args = [arr for pair in zip(outs, lses) for arr in pair]
    return pl.pallas_call(
        _merge_kernel,
        grid=(bsz, seq // TM),
        in_specs=[tok(D_MODEL), tok(2 * D_MODEL), tok(N_HEADS_B * HD_B)] + a_specs
                 + [_resident((GW_A, D_MODEL)), _resident((N_HEADS_B * HD_B, D_MODEL)),
                    _resident((D_MODEL, D_MODEL))],
        out_specs=tok(D_MODEL),
        out_shape=jax.ShapeDtypeStruct((bsz, seq, D_MODEL), F32),
        scratch_shapes=[pltpu.VMEM((2, GW_A // LANES, TM, LANES), F32),
                        pltpu.VMEM((2, GW_A // LANES, TM, LANES), F32)],
        compiler_params=pltpu.CompilerParams(dimension_semantics=("arbitrary", "arbitrary"),
                                             vmem_limit_bytes=VMEM_LIMIT),
        name="merge",
    )(x1, gates, ob, *a_args, wa, wb, wo)


def _t5_bucket(rel):
    n = N_BUCKETS // 2
    max_exact = n // 2
    ret = jnp.where(rel > 0, n, 0)
    a = jnp.abs(rel)
    af = jnp.maximum(a, 1).astype(F32)
    large = max_exact + (jnp.log(af / max_exact) / math.log(MAX_DISTANCE / max_exact)
                         * (n - max_exact)).astype(jnp.int32)
    large = jnp.minimum(large, n - 1)
    return ret + jnp.where(a < max_exact, a, large)


def _bucket_tiles(dilation, half):
    qi = jnp.arange(QBLK_A, dtype=jnp.int32)[:, None]
    kj = jnp.arange(KWIN_A, dtype=jnp.int32)[None, :]
    tiles = []
    for shift in (0, (KWIN_A - QBLK_A) // 2, KWIN_A - QBLK_A):
        rel = kj - shift - qi
        tiles.append(jnp.where(jnp.abs(rel) <= half, _t5_bucket(rel * dilation), -1))
    return jnp.stack(tiles)


def _w_in_layout():
    perm = list(range(0, HD_B, 2)) + list(range(1, HD_B, 2))
    cols, scale = [], []
    for g in range(N_GROUPS_A):
        for part in range(3):
            lo = (part * N_GROUPS_A + g) * GW_A
            cols += list(range(lo, lo + GW_A))
            scale += [HD_A ** -0.5 * LOG2E if part == 0 else 1.0] * GW_A
    for hh in range(N_HEADS_B + N_KV_B):
        cols += [OFF_QB + hh * HD_B + p for p in perm]
    cols += list(range(OFF_VB, IN_WIDTH))
    scale += [1.0] * (IN_WIDTH - OFF_QB)
    return perm, cols, scale


def kernel(x, ffn1_norm, ffn1_w1, ffn1_w3, ffn1_w2, mix_norm, w_in, b_gate, q_norm, k_norm, rel_bias,
           w_branch_a, w_branch_b, w_out, ffn2_norm, ffn2_w1, ffn2_w3, ffn2_w2, final_norm):
    bsz, seq, dm = x.shape
    assert dm == D_MODEL and seq % TM == 0 and ffn1_w1.shape[0] == 1
    row = lambda v: v.reshape(1, -1).astype(F32)
    perm, cols, scale = _w_in_layout()
    perm = jnp.asarray(perm, jnp.int32)

    t = jnp.arange(seq, dtype=jnp.int32)
    n_freq = HD_B // 4
    freq = ROPE_THETA ** (-jnp.arange(n_freq, dtype=F32) / n_freq)
    ang = jnp.concatenate([(t // GRID_W).astype(F32)[:, None] * freq, (t % GRID_W).astype(F32)[:, None] * freq], -1)
    cosf = jnp.concatenate([jnp.cos(ang), jnp.cos(ang)], -1)
    sinf = jnp.concatenate([-jnp.sin(ang), jnp.sin(ang)], -1)

    w_in_k = (w_in[0][:, jnp.asarray(cols, jnp.int32)] * jnp.asarray(scale, F32)).astype(BF16)
    qn = row(q_norm[0][perm] * (HD_B ** -0.5 * LOG2E))
    kn = row(k_norm[0][perm])

    x1 = _ffn(x.reshape(bsz * seq, dm), row(ffn1_norm[0]), ffn1_w1[0].astype(BF16), ffn1_w3[0].astype(BF16),
              ffn1_w2[0].astype(BF16), row(final_norm), False).reshape(bsz, seq, dm)
    a0, a1, a2, qb, kb, vt, gates = _proj(x1, row(mix_norm[0]), w_in_k, row(b_gate[0]), qn, kn, cosf, sinf)

    outs, lses = [], []
    for g, (a_g, (window, d)) in enumerate(zip((a0, a1, a2), DILATED_GROUPS)):
        table = rel_bias[:, g * HEADS_A:(g + 1) * HEADS_A].astype(F32)
        o_g, lse_g = _dilated(a_g, _bucket_tiles(d, window // (2 * d)), table, g)
        outs.append(o_g)
        lses.append(lse_g)
    ob = _gqa(qb, kb, vt)

    x2 = _merge(x1, gates, ob, outs, lses, w_branch_a[0].astype(BF16), w_branch_b[0].astype(BF16),
                w_out[0].astype(BF16))
    y = _ffn(x2.reshape(bsz * seq, dm), row(ffn2_norm[0]), ffn2_w1[0].astype(BF16), ffn2_w3[0].astype(BF16),
             ffn2_w2[0].astype(BF16), row(final_norm), True)
    return y.reshape(bsz, seq, dm)
```

```python
import functools
import math

import jax
import jax.numpy as jnp
from jax import lax
from jax.experimental import pallas as pl
from jax.experimental.pallas import tpu as pltpu

D_MODEL = 1024
D_FF = 2816
DILATED_GROUPS = ((128, 1), (512, 4), (2048, 16))
N_GROUPS_A = 3
HEADS_A = 8
HD_A = 64
GW_A = HEADS_A * HD_A
N_HEADS_B = 8
N_KV_B = 2
GQA_B = N_HEADS_B // N_KV_B
HD_B = 128
GRID_W = 64
ROPE_THETA = 10000.0
N_BUCKETS = 32
MAX_DISTANCE = 1024
EPS = 1e-6
NEG_INF = -1e30
LOG2E = 1.4426950408889634

LANES = 128
TM = 512
TM_FFN = 1024
FFN_CHUNK = 256
PROJ_CHUNK = 256
QBLK_A = 128
KWIN_A = 256
DILATED_UNROLL = 8
TQ_B = 256
KCH_B = 512
VMEM_LIMIT = 60 * 1024 * 1024

BF16 = jnp.bfloat16
F32 = jnp.float32


def _resident(shape):
    nd = len(shape)
    return pl.BlockSpec(shape, lambda *_: (0,) * nd, pipeline_mode=pl.Buffered(1))


def _rms(x, gain):
    ms = jnp.mean(x * x, axis=-1, keepdims=True)
    return x * lax.rsqrt(ms + EPS) * gain


def _ffn_kernel(x_ref, g_ref, w1_ref, w3_ref, w2_ref, fg_ref, o_ref, *, final_norm):
    x = x_ref[...]
    h = _rms(x, g_ref[...]).astype(BF16)
    acc = jnp.zeros((x.shape[0], D_MODEL), F32)
    for j in range(D_FF // FFN_CHUNK):
        c = slice(j * FFN_CHUNK, (j + 1) * FFN_CHUNK)
        a = jnp.dot(h, w1_ref[:, c], preferred_element_type=F32)
        b = jnp.dot(h, w3_ref[:, c], preferred_element_type=F32)
        g = (a * jax.nn.sigmoid(a) * b).astype(BF16)
        acc = acc + jnp.dot(g, w2_ref[c, :], preferred_element_type=F32)
    y = x + 0.5 * acc
    if final_norm:
        y = _rms(y, fg_ref[...])
    o_ref[...] = y


def _ffn(x2d, gain, w1, w3, w2, final_gain, final_norm):
    n = x2d.shape[0]
    assert n % TM_FFN == 0
    row = pl.BlockSpec((TM_FFN, D_MODEL), lambda i: (i, 0))
    return pl.pallas_call(
        functools.partial(_ffn_kernel, final_norm=final_norm),
        grid=(n // TM_FFN,),
        in_specs=[row, _resident((1, D_MODEL)), _resident((D_MODEL, D_FF)), _resident((D_MODEL, D_FF)),
                  _resident((D_FF, D_MODEL)), _resident((1, D_MODEL))],
        out_specs=row,
        out_shape=jax.ShapeDtypeStruct((n, D_MODEL), F32),
        compiler_params=pltpu.CompilerParams(dimension_semantics=("arbitrary",), vmem_limit_bytes=VMEM_LIMIT),
        name="ffn_final" if final_norm else "ffn",
    )(x2d, gain, w1, w3, w2, final_gain)


A_WIDTH = 3 * GW_A
OFF_QB = N_GROUPS_A * A_WIDTH
OFF_KB = OFF_QB + N_HEADS_B * HD_B
OFF_VB = OFF_KB + N_KV_B * HD_B
OFF_GATE = OFF_VB + N_KV_B * HD_B
IN_WIDTH = OFF_GATE + 2 * D_MODEL


def _rope(y, cosf, sinf):
    return y * cosf + pltpu.roll(y, HD_B // 2, axis=1) * sinf


def _proj_kernel(x_ref, g_ref, w_ref, bg_ref, qn_ref, kn_ref, cos_ref, sin_ref,
                 a0_ref, a1_ref, a2_ref, qb_ref, kb_ref, vt_ref, gate_ref, slab_ref, h_ref):
    h_ref[...] = _rms(x_ref[...], g_ref[...]).astype(BF16)
    tm = h_ref.shape[0]

    def mm(lo):
        return jnp.dot(h_ref[...], w_ref[:, lo:lo + PROJ_CHUNK], preferred_element_type=F32)

    for j in range(2 * D_MODEL // PROJ_CHUNK):
        c = slice(j * PROJ_CHUNK, (j + 1) * PROJ_CHUNK)
        pg = mm(OFF_GATE + j * PROJ_CHUNK) + bg_ref[:, c]
        gate_ref[:, c] = (0.5 * jnp.tanh(0.5 * pg) + 0.5).astype(BF16)

    cosf = cos_ref[...]
    sinf = sin_ref[...]
    heads_per_chunk = PROJ_CHUNK // HD_B
    for j in range(N_HEADS_B // heads_per_chunk):
        pq = mm(OFF_QB + j * PROJ_CHUNK)
        for hh in range(heads_per_chunk):
            src = slice(hh * HD_B, (hh + 1) * HD_B)
            dst = slice(j * PROJ_CHUNK + hh * HD_B, j * PROJ_CHUNK + (hh + 1) * HD_B)
            qb_ref[:, dst] = _rope(_rms(pq[:, src], qn_ref[...]), cosf, sinf).astype(BF16)
    pk = mm(OFF_KB)
    for hh in range(N_KV_B):
        c = slice(hh * HD_B, (hh + 1) * HD_B)
        kb_ref[:, c] = _rope(_rms(pk[:, c], kn_ref[...]), cosf, sinf).astype(BF16)
    vt_ref[...] = mm(OFF_VB).T.astype(BF16)

    for g, a_ref in ((2, a2_ref), (1, a1_ref)):
        d = DILATED_GROUPS[g][1]
        for j in range(A_WIDTH // PROJ_CHUNK):
            p = mm(g * A_WIDTH + j * PROJ_CHUNK)
            for s in range(PROJ_CHUNK // LANES):
                slab = j * (PROJ_CHUNK // LANES) + s
                slab_ref[g - 1, slab] = p[:, s * LANES:(s + 1) * LANES]
                for r in range(d):
                    a_ref[r, :, slab * LANES:(slab + 1) * LANES] = (
                        slab_ref[g - 1, slab, pl.ds(r, tm // d, stride=d), :].astype(BF16))
    for j in range(A_WIDTH // PROJ_CHUNK):
        c = slice(j * PROJ_CHUNK, (j + 1) * PROJ_CHUNK)
        a0_ref[:, c] = mm(j * PROJ_CHUNK).astype(BF16)


def _proj(x1, gain, w_in, b_gate, qn, kn, cosf, sinf):
    bsz, seq, _ = x1.shape
    nt = seq // TM
    outs, specs = [], []
    for _, d in DILATED_GROUPS:
        outs.append(jax.ShapeDtypeStruct((bsz, d, seq // d, A_WIDTH), BF16))
        specs.append(pl.BlockSpec((None, d, TM // d, A_WIDTH), lambda b, t: (b, 0, t, 0)))
    outs += [jax.ShapeDtypeStruct((bsz, seq, N_HEADS_B * HD_B), BF16),
             jax.ShapeDtypeStruct((bsz, seq, N_KV_B * HD_B), BF16),
             jax.ShapeDtypeStruct((bsz, N_KV_B * HD_B, seq), BF16),
             jax.ShapeDtypeStruct((bsz, seq, 2 * D_MODEL), BF16)]
    specs += [pl.BlockSpec((None, TM, N_HEADS_B * HD_B), lambda b, t: (b, t, 0)),
              pl.BlockSpec((None, TM, N_KV_B * HD_B), lambda b, t: (b, t, 0)),
              pl.BlockSpec((None, N_KV_B * HD_B, TM), lambda b, t: (b, 0, t)),
              pl.BlockSpec((None, TM, 2 * D_MODEL), lambda b, t: (b, t, 0))]
    specs[0] = pl.BlockSpec((None, None, TM, A_WIDTH), lambda b, t: (b, 0, t, 0))
    return pl.pallas_call(
        _proj_kernel,
        grid=(bsz, nt),
        in_specs=[pl.BlockSpec((None, TM, D_MODEL), lambda b, t: (b, t, 0)),
                  _resident((1, D_MODEL)), _resident((D_MODEL, IN_WIDTH)), _resident((1, 2 * D_MODEL)),
                  _resident((1, HD_B)), _resident((1, HD_B)),
                  pl.BlockSpec((TM, HD_B), lambda b, t: (t, 0)),
                  pl.BlockSpec((TM, HD_B), lambda b, t: (t, 0))],
        out_specs=specs,
        out_shape=outs,
        scratch_shapes=[pltpu.VMEM((2, A_WIDTH // LANES, TM, LANES), F32), pltpu.VMEM((TM, D_MODEL), BF16)],
        compiler_params=pltpu.CompilerParams(dimension_semantics=("arbitrary", "arbitrary"),
                                             vmem_limit_bytes=VMEM_LIMIT),
        name="proj",
    )(x1, gain, w_in, b_gate, qn, kn, cosf, sinf)


def _dilated_kernel(bucket_ref, tab_ref, q_ref, k_ref, v_ref, o_ref, st_ref, bias_ref, s_ref, *, sub_len):
    nblk = sub_len // QBLK_A
    npair = HEADS_A // 2

    @pl.when((pl.program_id(0) == 0) & (pl.program_id(1) == 0))
    def _build_bias():
        for v in range(3):
            bt = bucket_ref[v]
            for hh in range(HEADS_A):
                acc = jnp.full((QBLK_A, KWIN_A), NEG_INF, F32)
                for j in range(N_BUCKETS):
                    acc = jnp.where(bt == j, tab_ref[j, hh] * LOG2E, acc)
                bias_ref[v, hh // 2, (hh % 2) * QBLK_A:(hh % 2 + 1) * QBLK_A, :] = acc

    lane = lax.broadcasted_iota(jnp.int32, (QBLK_A, LANES), 1)
    low = lane < HD_A

    def geometry(n):
        q0 = pl.multiple_of(n * QBLK_A, QBLK_A)
        w0 = pl.multiple_of(jnp.clip(n * QBLK_A - (KWIN_A - QBLK_A) // 2, 0, sub_len - KWIN_A), 64)
        variant = jnp.where(n == 0, 0, jnp.where(n == nblk - 1, 2, 1))
        return q0, w0, variant

    def score_pass(n, pair):
        q0, w0, variant = geometry(n)
        c = slice(pair * LANES, (pair + 1) * LANES)
        qp = q_ref[pl.ds(q0, QBLK_A), c]
        kp = k_ref[pl.ds(w0, KWIN_A), c]
        zero = jnp.zeros_like(qp)
        q2 = jnp.concatenate([jnp.where(low, qp, zero), jnp.where(low, zero, qp)], axis=0)
        s = lax.dot_general(q2, kp, (((1,), (1,)), ((), ())), preferred_element_type=F32)
        s = s + bias_ref[variant, pair]
        s_ref[pair] = s
        return jnp.max(s, axis=-1, keepdims=True)

    def value_pass(n, pair, m, stats):
        q0, w0, _ = geometry(n)
        c = slice(pair * LANES, (pair + 1) * LANES)
        vp = v_ref[pl.ds(w0, KWIN_A), c]
        p = jnp.exp2(s_ref[pair] - m)
        l = jnp.sum(p, axis=-1, keepdims=True)
        o = jnp.dot(p.astype(BF16), vp, preferred_element_type=F32)
        o_ref[pl.ds(q0, QBLK_A), c] = jnp.where(low, o[:QBLK_A], o[QBLK_A:]).astype(BF16)
        for half in range(2):
            rows = slice(half * QBLK_A, (half + 1) * QBLK_A)
            hh = 2 * pair + half
            stats = jnp.where(lane == hh, m[rows], stats)
            stats = jnp.where(lane == HEADS_A + hh, l[rows], stats)
        return stats

    def body(n, ms):
        stats = jnp.zeros((QBLK_A, LANES), F32)
        for pair in range(npair):
            if pair + 1 < npair:
                nxt = score_pass(n, pair + 1)
            else:
                nxt = score_pass(jnp.minimum(n + 1, nblk - 1), 0)
            stats = value_pass(n, pair, ms, stats)
            ms = nxt
        st_ref[pl.ds(pl.multiple_of(n * QBLK_A, QBLK_A), QBLK_A), :] = stats
        return ms

    lax.fori_loop(0, nblk, body, score_pass(0, 0), unroll=DILATED_UNROLL)


def _dilated(a_g, buckets, table, g):
    bsz, d, sub_len, _ = a_g.shape
    assert sub_len % QBLK_A == 0 and sub_len >= KWIN_A
    blk = lambda col: pl.BlockSpec((None, None, sub_len, GW_A), lambda b, r: (b, r, 0, col))
    return pl.pallas_call(
        functools.partial(_dilated_kernel, sub_len=sub_len),
        grid=(bsz, d),
        in_specs=[_resident((3, QBLK_A, KWIN_A)),
                  pl.BlockSpec(memory_space=pltpu.SMEM),
                  blk(0), blk(1), blk(2)],
        out_specs=[blk(0), pl.BlockSpec((None, None, sub_len, LANES), lambda b, r: (b, r, 0, 0))],
        out_shape=[jax.ShapeDtypeStruct((bsz, d, sub_len, GW_A), BF16),
                   jax.ShapeDtypeStruct((bsz, d, sub_len, LANES), F32)],
        scratch_shapes=[pltpu.VMEM((3, HEADS_A // 2, 2 * QBLK_A, KWIN_A), F32),
                        pltpu.VMEM((HEADS_A // 2, 2 * QBLK_A, KWIN_A), F32)],
        compiler_params=pltpu.CompilerParams(dimension_semantics=("arbitrary", "arbitrary"),
                                             vmem_limit_bytes=VMEM_LIMIT),
        name=f"dilated{g}",
    )(buckets, table, a_g, a_g, a_g)


def _gqa_kernel(q_ref, k_ref, vt_ref, o_ref, s_ref, p_ref):
    seq = k_ref.shape[0]
    nch = seq // KCH_B
    nqt = seq // TQ_B

    def score_pass(qt, g):
        q0 = pl.multiple_of(qt * TQ_B, TQ_B)
        q = q_ref[pl.ds(q0, TQ_B), g * HD_B:(g + 1) * HD_B]
        mrun = jnp.full((8, TQ_B), NEG_INF, F32)
        for ch in range(nch):
            rows = slice(ch * KCH_B, (ch + 1) * KCH_B)
            st = lax.dot_general(k_ref[rows, :], q, (((1,), (1,)), ((), ())),
                                 preferred_element_type=F32)
            s_ref[g, rows, :] = st
            mrun = jnp.maximum(mrun, jnp.max(st.reshape(KCH_B // 8, 8, TQ_B), axis=0))
        return mrun

    def value_pass(qt, g, mrun):
        q0 = pl.multiple_of(qt * TQ_B, TQ_B)
        m = jnp.max(mrun, axis=0, keepdims=True)
        lrun = jnp.zeros((8, TQ_B), F32)
        for ch in range(nch):
            rows = slice(ch * KCH_B, (ch + 1) * KCH_B)
            p = jnp.exp2(s_ref[g, rows, :] - m)
            lrun = lrun + jnp.sum(p.reshape(KCH_B // 8, 8, TQ_B), axis=0)
            p_ref[g, rows, :] = p.astype(BF16)
        l = jnp.sum(lrun, axis=0, keepdims=True)
        ot = jnp.dot(vt_ref[...], p_ref[g], preferred_element_type=F32)
        o_ref[pl.ds(q0, TQ_B), g * HD_B:(g + 1) * HD_B] = (ot / l).T.astype(BF16)

    def body(qt, mrun):
        for g in range(GQA_B):
            if g + 1 < GQA_B:
                nxt = score_pass(qt, g + 1)
            else:
                nxt = score_pass(jnp.minimum(qt + 1, nqt - 1), 0)
            value_pass(qt, g, mrun)
            mrun = nxt
        return mrun

    lax.fori_loop(0, nqt, body, score_pass(0, 0))


def _gqa(qb, kb, vt):
    bsz, seq, _ = qb.shape
    gw = GQA_B * HD_B
    return pl.pallas_call(
        _gqa_kernel,
        grid=(bsz, N_KV_B),
        in_specs=[pl.BlockSpec((None, seq, gw), lambda b, kv: (b, 0, kv)),
                  pl.BlockSpec((None, seq, HD_B), lambda b, kv: (b, 0, kv)),
                  pl.BlockSpec((None, HD_B, seq), lambda b, kv: (b, kv, 0))],
        out_specs=pl.BlockSpec((None, seq, gw), lambda b, kv: (b, 0, kv)),
        out_shape=jax.ShapeDtypeStruct((bsz, seq, N_HEADS_B * HD_B), BF16),
        scratch_shapes=[pltpu.VMEM((GQA_B, seq, TQ_B), F32), pltpu.VMEM((GQA_B, seq, TQ_B), BF16)],
        compiler_params=pltpu.CompilerParams(dimension_semantics=("arbitrary", "arbitrary"),
                                             vmem_limit_bytes=VMEM_LIMIT),
        name="gqa",
    )(qb, kb, vt)


def _merge_kernel(x_ref, gate_ref, ob_ref, o0_ref, s0_ref, o1_ref, s1_ref, o2_ref, s2_ref,
                  wa_ref, wb_ref, wo_ref, expand_ref, out_ref, oslab_ref, sslab_ref):
    tm = x_ref.shape[0]
    nslab = GW_A // LANES

    for g, (o_ref, s_ref) in enumerate(((o1_ref, s1_ref), (o2_ref, s2_ref))):
        d = DILATED_GROUPS[g + 1][1]
        for r in range(d):
            sslab_ref[g, pl.ds(r, tm // d, stride=d), :] = s_ref[r]
            for s in range(nslab):
                c = slice(s * LANES, (s + 1) * LANES)
                oslab_ref[g, s, pl.ds(r, tm // d, stride=d), :] = o_ref[r, :, c].astype(F32)

    stats = [s0_ref[...], sslab_ref[0], sslab_ref[1]]
    dens = [pltpu.roll(st, LANES - HEADS_A, axis=1) for st in stats]
    mx = jnp.maximum(jnp.maximum(stats[0], stats[1]), stats[2])
    es = [jnp.exp2(st - mx) for st in stats]
    total = es[0] * dens[0] + es[1] * dens[1] + es[2] * dens[2]
    head_lane = lax.broadcasted_iota(jnp.int32, (tm, LANES), 1) < HEADS_A
    o_a = jnp.zeros((tm, GW_A), F32)
    for g in range(N_GROUPS_A):
        w = jnp.where(head_lane, es[g] / total, 0.0)
        w_hi = w.astype(BF16)
        w_lo = (w - w_hi.astype(F32)).astype(BF16)
        wide = jnp.dot(jnp.concatenate([w_hi, w_lo], axis=1), expand_ref[...], preferred_element_type=F32)
        if g == 0:
            acc = o0_ref[...].astype(F32)
        else:
            acc = jnp.concatenate([oslab_ref[g - 1, s] for s in range(nslab)], axis=1)
        o_a = o_a + wide * acc
    o_a = o_a.astype(BF16)

    ya = jnp.dot(o_a, wa_ref[...], preferred_element_type=F32)
    yb = jnp.dot(ob_ref[...], wb_ref[...], preferred_element_type=F32)
    merged = (gate_ref[:, :D_MODEL].astype(F32) * ya + gate_ref[:, D_MODEL:].astype(F32) * yb).astype(BF16)
    out_ref[...] = x_ref[...] + jnp.dot(merged, wo_ref[...], preferred_element_type=F32)


def _merge(x1, gates, ob, outs, stats, wa, wb, wo):
    bsz, seq, _ = x1.shape
    tok = lambda w: pl.BlockSpec((None, TM, w), lambda b, t: (b, t, 0))
    a_specs = []
    for _, d in DILATED_GROUPS:
        for width in (GW_A, LANES):
            if d == 1:
                a_specs.append(pl.BlockSpec((None, None, TM, width), lambda b, t: (b, 0, t, 0)))
            else:
                a_specs.append(pl.BlockSpec((None, d, TM // d, width), lambda b, t: (b, 0, t, 0)))
    a_args = [arr for pair in zip(outs, stats) for arr in pair]
    src = jnp.arange(2 * LANES, dtype=jnp.int32)[:, None] % LANES
    dst = jnp.arange(GW_A, dtype=jnp.int32)[None, :] // HD_A
    expand = (src == dst).astype(BF16)
    return pl.pallas_call(
        _merge_kernel,
        grid=(bsz, seq // TM),
        in_specs=[tok(D_MODEL), tok(2 * D_MODEL), tok(N_HEADS_B * HD_B)] + a_specs
                 + [_resident((GW_A, D_MODEL)), _resident((N_HEADS_B * HD_B, D_MODEL)),
                    _resident((D_MODEL, D_MODEL)), _resident((2 * LANES, GW_A))],
        out_specs=tok(D_MODEL),
        out_shape=jax.ShapeDtypeStruct((bsz, seq, D_MODEL), F32),
        scratch_shapes=[pltpu.VMEM((2, GW_A // LANES, TM, LANES), F32),
                        pltpu.VMEM((2, TM, LANES), F32)],
        compiler_params=pltpu.CompilerParams(dimension_semantics=("arbitrary", "arbitrary"),
                                             vmem_limit_bytes=VMEM_LIMIT),
        name="merge",
    )(x1, gates, ob, *a_args, wa, wb, wo, expand)


def _t5_bucket(rel):
    n = N_BUCKETS // 2
    max_exact = n // 2
    ret = jnp.where(rel > 0, n, 0)
    a = jnp.abs(rel)
    af = jnp.maximum(a, 1).astype(F32)
    large = max_exact + (jnp.log(af / max_exact) / math.log(MAX_DISTANCE / max_exact)
                         * (n - max_exact)).astype(jnp.int32)
    large = jnp.minimum(large, n - 1)
    return ret + jnp.where(a < max_exact, a, large)


def _bucket_tiles(dilation, half):
    qi = jnp.arange(QBLK_A, dtype=jnp.int32)[:, None]
    kj = jnp.arange(KWIN_A, dtype=jnp.int32)[None, :]
    tiles = []
    for shift in (0, (KWIN_A - QBLK_A) // 2, KWIN_A - QBLK_A):
        rel = kj - shift - qi
        tiles.append(jnp.where(jnp.abs(rel) <= half, _t5_bucket(rel * dilation), -1))
    return jnp.stack(tiles)


def _w_in_layout():
    perm = list(range(0, HD_B, 2)) + list(range(1, HD_B, 2))
    cols, scale = [], []
    for g in range(N_GROUPS_A):
        for part in range(3):
            lo = (part * N_GROUPS_A + g) * GW_A
            cols += list(range(lo, lo + GW_A))
            scale += [HD_A ** -0.5 * LOG2E if part == 0 else 1.0] * GW_A
    for hh in range(N_HEADS_B + N_KV_B):
        cols += [OFF_QB + hh * HD_B + p for p in perm]
    cols += list(range(OFF_VB, IN_WIDTH))
    scale += [1.0] * (IN_WIDTH - OFF_QB)
    return perm, cols, scale


def kernel(x, ffn1_norm, ffn1_w1, ffn1_w3, ffn1_w2, mix_norm, w_in, b_gate, q_norm, k_norm, rel_bias,
           w_branch_a, w_branch_b, w_out, ffn2_norm, ffn2_w1, ffn2_w3, ffn2_w2, final_norm):
    bsz, seq, dm = x.shape
    assert dm == D_MODEL and seq % TM == 0 and ffn1_w1.shape[0] == 1
    row = lambda v: v.reshape(1, -1).astype(F32)
    perm, cols, scale = _w_in_layout()
    perm = jnp.asarray(perm, jnp.int32)

    t = jnp.arange(seq, dtype=jnp.int32)
    n_freq = HD_B // 4
    freq = ROPE_THETA ** (-jnp.arange(n_freq, dtype=F32) / n_freq)
    ang = jnp.concatenate([(t // GRID_W).astype(F32)[:, None] * freq, (t % GRID_W).astype(F32)[:, None] * freq], -1)
    cosf = jnp.concatenate([jnp.cos(ang), jnp.cos(ang)], -1)
    sinf = jnp.concatenate([-jnp.sin(ang), jnp.sin(ang)], -1)

    w_in_k = (w_in[0][:, jnp.asarray(cols, jnp.int32)] * jnp.asarray(scale, F32)).astype(BF16)
    qn = row(q_norm[0][perm] * (HD_B ** -0.5 * LOG2E))
    kn = row(k_norm[0][perm])

    x1 = _ffn(x.reshape(bsz * seq, dm), row(ffn1_norm[0]), ffn1_w1[0].astype(BF16), ffn1_w3[0].astype(BF16),
              ffn1_w2[0].astype(BF16), row(final_norm), False).reshape(bsz, seq, dm)
    a0, a1, a2, qb, kb, vt, gates = _proj(x1, row(mix_norm[0]), w_in_k, row(b_gate[0]), qn, kn, cosf, sinf)

    outs, stats = [], []
    for g, (a_g, (window, d)) in enumerate(zip((a0, a1, a2), DILATED_GROUPS)):
        table = rel_bias[:, g * HEADS_A:(g + 1) * HEADS_A].astype(F32)
        o_g, st_g = _dilated(a_g, _bucket_tiles(d, window // (2 * d)), table, g)
        outs.append(o_g)
        stats.append(st_g)
    ob = _gqa(qb, kb, vt)

    x2 = _merge(x1, gates, ob, outs, stats, w_branch_a[0].astype(BF16), w_branch_b[0].astype(BF16),
                w_out[0].astype(BF16))
    y = _ffn(x2.reshape(bsz * seq, dm), row(ffn2_norm[0]), ffn2_w1[0].astype(BF16), ffn2_w3[0].astype(BF16),
             ffn2_w2[0].astype(BF16), row(final_norm), True)
    return y.reshape(bsz, seq, dm)
```

```python
import functools
import math

import jax
import jax.numpy as jnp
import numpy as np
from jax import lax
from jax.experimental import pallas as pl
from jax.experimental.pallas import tpu as pltpu

D_MODEL = 1024
D_FF = 2816
DILATED_GROUPS = ((128, 1), (512, 4), (2048, 16))
N_GROUPS_A = 3
HEADS_A = 8
HD_A = 64
GW_A = HEADS_A * HD_A
N_HEADS_B = 8
N_KV_B = 2
GQA_B = N_HEADS_B // N_KV_B
HD_B = 128
GRID_W = 64
ROPE_THETA = 10000.0
N_BUCKETS = 32
MAX_DISTANCE = 1024
EPS = 1e-6
NEG_INF = -1e30
LOG2E = 1.4426950408889634

LANES = 128
TM = 512
TM_FFN = 1024
FFN_CHUNK = 256
PROJ_CHUNK = 256
QBLK_A = 128
KWIN_A = 256
DILATED_ROWS = 4096
DILATED_UNROLL = 8
TQ_B = 256
KCH_B = 512
GQA_UNROLL = 2
VMEM_LIMIT = 60 * 1024 * 1024

BF16 = jnp.bfloat16
F32 = jnp.float32


def _resident(shape):
    nd = len(shape)
    return pl.BlockSpec(shape, lambda *_: (0,) * nd, pipeline_mode=pl.Buffered(1))


def _rms(x, gain):
    ms = jnp.mean(x * x, axis=-1, keepdims=True)
    return x * lax.rsqrt(ms + EPS) * gain


def _ffn_kernel(x_ref, g_ref, w1_ref, w3_ref, w2_ref, fg_ref, o_ref, *, final_norm):
    x = x_ref[...]
    h = _rms(x, g_ref[...]).astype(BF16)
    acc = jnp.zeros((x.shape[0], D_MODEL), F32)
    for j in range(D_FF // FFN_CHUNK):
        c = slice(j * FFN_CHUNK, (j + 1) * FFN_CHUNK)
        a = jnp.dot(h, w1_ref[:, c], preferred_element_type=F32)
        b = jnp.dot(h, w3_ref[:, c], preferred_element_type=F32)
        g = (a * jax.nn.sigmoid(a) * b).astype(BF16)
        acc = acc + jnp.dot(g, w2_ref[c, :], preferred_element_type=F32)
    y = x + 0.5 * acc
    if final_norm:
        y = _rms(y, fg_ref[...])
    o_ref[...] = y


def _ffn(x2d, gain, w1, w3, w2, final_gain, final_norm):
    n = x2d.shape[0]
    assert n % TM_FFN == 0
    row = pl.BlockSpec((TM_FFN, D_MODEL), lambda i: (i, 0))
    return pl.pallas_call(
        functools.partial(_ffn_kernel, final_norm=final_norm),
        grid=(n // TM_FFN,),
        in_specs=[row, _resident((1, D_MODEL)), _resident((D_MODEL, D_FF)), _resident((D_MODEL, D_FF)),
                  _resident((D_FF, D_MODEL)), _resident((1, D_MODEL))],
        out_specs=row,
        out_shape=jax.ShapeDtypeStruct((n, D_MODEL), F32),
        compiler_params=pltpu.CompilerParams(dimension_semantics=("arbitrary",), vmem_limit_bytes=VMEM_LIMIT),
        name="ffn_final" if final_norm else "ffn",
    )(x2d, gain, w1, w3, w2, final_gain)


A_WIDTH = 3 * GW_A
QSCALE_A = HD_A ** -0.5 * LOG2E
OFF_QB = N_GROUPS_A * A_WIDTH
OFF_KB = OFF_QB + N_HEADS_B * HD_B
OFF_VB = OFF_KB + N_KV_B * HD_B
OFF_GATE = OFF_VB + N_KV_B * HD_B
IN_WIDTH = OFF_GATE + 2 * D_MODEL


def _proj_kernel(x_ref, g_ref, w_ref, bg_ref, qn_ref, kn_ref, cos_ref, sin_ref,
                 a0_ref, a1_ref, a2_ref, qb_ref, kb_ref, vt_ref, gate_ref, hs_ref, hb_ref):
    tm = x_ref.shape[0]
    nslab = D_MODEL // LANES
    even_lane = (lax.broadcasted_iota(jnp.int32, (tm, HD_B), 1) & 1) == 0

    hn = _rms(x_ref[...], g_ref[...])
    hb_ref[0] = hn.astype(BF16)
    for s in range(nslab):
        hs_ref[0, s] = hn[:, s * LANES:(s + 1) * LANES]
    d_prev = 1
    for g in range(1, N_GROUPS_A):
        d = DILATED_GROUPS[g][1]
        f, lp, ln = d // d_prev, tm // d_prev, tm // d
        for r_prev in range(d_prev):
            for q in range(f):
                r = r_prev + d_prev * q
                for s in range(nslab):
                    rows = hs_ref[(g - 1) % 2, s, pl.ds(r_prev * lp + q, ln, stride=f), :]
                    hb_ref[g, r * ln:(r + 1) * ln, s * LANES:(s + 1) * LANES] = rows.astype(BF16)
                    if g + 1 < N_GROUPS_A:
                        hs_ref[g % 2, s, r * ln:(r + 1) * ln, :] = rows
        d_prev = d

    def mm(lo, g=0):
        return jnp.dot(hb_ref[g], w_ref[:, lo:lo + PROJ_CHUNK], preferred_element_type=F32)

    def _rope(y, cosf, sinf):
        partner = jnp.where(even_lane, pltpu.roll(y, HD_B - 1, axis=1), pltpu.roll(y, 1, axis=1))
        return y * cosf + partner * sinf

    def a_chunk(g, j):
        part, half = divmod(j, GW_A // PROJ_CHUNK)
        p = mm((part * N_GROUPS_A + g) * GW_A + half * PROJ_CHUNK, g)
        return p * QSCALE_A if part == 0 else p

    for j in range(2 * D_MODEL // PROJ_CHUNK):
        c = slice(j * PROJ_CHUNK, (j + 1) * PROJ_CHUNK)
        pg = mm(OFF_GATE + j * PROJ_CHUNK) + bg_ref[:, c]
        gate_ref[:, c] = (0.5 * jnp.tanh(0.5 * pg) + 0.5).astype(BF16)

    cosf = cos_ref[...]
    sinf = sin_ref[...]
    heads_per_chunk = PROJ_CHUNK // HD_B
    for j in range(N_HEADS_B // heads_per_chunk):
        pq = mm(OFF_QB + j * PROJ_CHUNK)
        for hh in range(heads_per_chunk):
            src = slice(hh * HD_B, (hh + 1) * HD_B)
            dst = slice(j * PROJ_CHUNK + hh * HD_B, j * PROJ_CHUNK + (hh + 1) * HD_B)
            qb_ref[:, dst] = _rope(_rms(pq[:, src], qn_ref[...]), cosf, sinf).astype(BF16)
    pk = mm(OFF_KB)
    for hh in range(N_KV_B):
        c = slice(hh * HD_B, (hh + 1) * HD_B)
        kb_ref[:, c] = _rope(_rms(pk[:, c], kn_ref[...]), cosf, sinf).astype(BF16)
    vt_ref[...] = mm(OFF_VB).T.astype(BF16)

    for g, a_ref in ((2, a2_ref), (1, a1_ref)):
        d = DILATED_GROUPS[g][1]
        for j in range(A_WIDTH // PROJ_CHUNK):
            c = slice(j * PROJ_CHUNK, (j + 1) * PROJ_CHUNK)
            a_ref[:, :, c] = a_chunk(g, j).reshape(d, tm // d, PROJ_CHUNK).astype(BF16)
    for j in range(A_WIDTH // PROJ_CHUNK):
        c = slice(j * PROJ_CHUNK, (j + 1) * PROJ_CHUNK)
        a0_ref[:, c] = a_chunk(0, j).astype(BF16)


def _proj(x1, gain, w_in, b_gate, qn, kn, cosf, sinf):
    bsz, seq, _ = x1.shape
    nt = seq // TM
    outs, specs = [], []
    for _, d in DILATED_GROUPS:
        outs.append(jax.ShapeDtypeStruct((bsz, d, seq // d, A_WIDTH), BF16))
        specs.append(pl.BlockSpec((None, d, TM // d, A_WIDTH), lambda b, t: (b, 0, t, 0)))
    outs += [jax.ShapeDtypeStruct((bsz, seq, N_HEADS_B * HD_B), BF16),
             jax.ShapeDtypeStruct((bsz, seq, N_KV_B * HD_B), BF16),
             jax.ShapeDtypeStruct((bsz, N_KV_B * HD_B, seq), BF16),
             jax.ShapeDtypeStruct((bsz, seq, 2 * D_MODEL), BF16)]
    specs += [pl.BlockSpec((None, TM, N_HEADS_B * HD_B), lambda b, t: (b, t, 0)),
              pl.BlockSpec((None, TM, N_KV_B * HD_B), lambda b, t: (b, t, 0)),
              pl.BlockSpec((None, N_KV_B * HD_B, TM), lambda b, t: (b, 0, t)),
              pl.BlockSpec((None, TM, 2 * D_MODEL), lambda b, t: (b, t, 0))]
    specs[0] = pl.BlockSpec((None, None, TM, A_WIDTH), lambda b, t: (b, 0, t, 0))
    return pl.pallas_call(
        _proj_kernel,
        grid=(bsz, nt),
        in_specs=[pl.BlockSpec((None, TM, D_MODEL), lambda b, t: (b, t, 0)),
                  _resident((1, D_MODEL)), _resident((D_MODEL, IN_WIDTH)), _resident((1, 2 * D_MODEL)),
                  _resident((1, HD_B)), _resident((1, HD_B)),
                  pl.BlockSpec((TM, HD_B), lambda b, t: (t, 0)),
                  pl.BlockSpec((TM, HD_B), lambda b, t: (t, 0))],
        out_specs=specs,
        out_shape=outs,
        scratch_shapes=[pltpu.VMEM((2, D_MODEL // LANES, TM, LANES), F32),
                        pltpu.VMEM((N_GROUPS_A, TM, D_MODEL), BF16)],
        compiler_params=pltpu.CompilerParams(dimension_semantics=("arbitrary", "arbitrary"),
                                             vmem_limit_bytes=VMEM_LIMIT),
        name="proj",
    )(x1, gain, w_in, b_gate, qn, kn, cosf, sinf)


def _dilated_kernel(bucket_ref, tab_ref, q_ref, k_ref, v_ref, o_ref, st_ref, bias_ref, s_ref, *, sub_len):
    nblk = sub_len // QBLK_A
    nunit = q_ref.shape[0] * nblk
    npair = HEADS_A // 2

    @pl.when((pl.program_id(0) == 0) & (pl.program_id(1) == 0))
    def _build_bias():
        for v in range(3):
            bt = bucket_ref[v]
            for hh in range(HEADS_A):
                acc = jnp.full((QBLK_A, KWIN_A), NEG_INF, F32)
                for j in range(N_BUCKETS):
                    acc = jnp.where(bt == j, tab_ref[j, hh] * LOG2E, acc)
                bias_ref[v, hh // 2, (hh % 2) * QBLK_A:(hh % 2 + 1) * QBLK_A, :] = acc

    lane = lax.broadcasted_iota(jnp.int32, (QBLK_A, LANES), 1)
    low = lane < HD_A

    def geometry(u):
        r = u >> (nblk.bit_length() - 1)
        n = u & (nblk - 1)
        q0 = pl.multiple_of(n * QBLK_A, QBLK_A)
        w0 = pl.multiple_of(jnp.clip(n * QBLK_A - (KWIN_A - QBLK_A) // 2, 0, sub_len - KWIN_A), 64)
        variant = jnp.where(n == 0, 0, jnp.where(n == nblk - 1, 2, 1))
        return r, q0, w0, variant

    def score_pass(u, pair):
        r, q0, w0, variant = geometry(u)
        c = slice(pair * LANES, (pair + 1) * LANES)
        qp = q_ref[r, pl.ds(q0, QBLK_A), c]
        kp = k_ref[r, pl.ds(w0, KWIN_A), c]
        zero = jnp.zeros_like(qp)
        q2 = jnp.concatenate([jnp.where(low, qp, zero), jnp.where(low, zero, qp)], axis=0)
        s = lax.dot_general(q2, kp, (((1,), (1,)), ((), ())), preferred_element_type=F32)
        s = s + bias_ref[variant, pair]
        s_ref[pair] = s
        return jnp.max(s, axis=-1, keepdims=True)

    def value_pass(u, pair, m, stats):
        r, q0, w0, _ = geometry(u)
        c = slice(pair * LANES, (pair + 1) * LANES)
        vp = v_ref[r, pl.ds(w0, KWIN_A), c]
        p = jnp.exp2(s_ref[pair] - m)
        l = jnp.sum(p, axis=-1, keepdims=True)
        o = jnp.dot(p.astype(BF16), vp, preferred_element_type=F32)
        o_ref[r, pl.ds(q0, QBLK_A), c] = jnp.where(low, o[:QBLK_A], o[QBLK_A:]).astype(BF16)
        for half in range(2):
            rows = slice(half * QBLK_A, (half + 1) * QBLK_A)
            hh = 2 * pair + half
            stats = jnp.where(lane == hh, m[rows], stats)
            stats = jnp.where(lane == HEADS_A + hh, l[rows], stats)
        return stats

    def body(u, ms):
        stats = jnp.zeros((QBLK_A, LANES), F32)
        for pair in range(npair):
            if pair + 1 < npair:
                nxt = score_pass(u, pair + 1)
            else:
                nxt = score_pass(jnp.minimum(u + 1, nunit - 1), 0)
            stats = value_pass(u, pair, ms, stats)
            ms = nxt
        r, q0, _, _ = geometry(u)
        st_ref[r, pl.ds(q0, QBLK_A), :] = stats
        return ms

    lax.fori_loop(0, nunit, body, score_pass(0, 0), unroll=DILATED_UNROLL)


def _dilated(a_g, buckets, table, g):
    bsz, d, sub_len, _ = a_g.shape
    nblk = sub_len // QBLK_A
    assert sub_len % QBLK_A == 0 and sub_len >= KWIN_A and nblk & (nblk - 1) == 0
    rb = min(d, max(1, DILATED_ROWS // sub_len))
    assert d % rb == 0
    blk = lambda col: pl.BlockSpec((None, rb, sub_len, GW_A), lambda b, r: (b, r, 0, col))
    return pl.pallas_call(
        functools.partial(_dilated_kernel, sub_len=sub_len),
        grid=(bsz, d // rb),
        in_specs=[_resident((3, QBLK_A, KWIN_A)),
                  pl.BlockSpec(memory_space=pltpu.SMEM),
                  blk(0), blk(1), blk(2)],
        out_specs=[blk(0), pl.BlockSpec((None, rb, sub_len, LANES), lambda b, r: (b, r, 0, 0))],
        out_shape=[jax.ShapeDtypeStruct((bsz, d, sub_len, GW_A), BF16),
                   jax.ShapeDtypeStruct((bsz, d, sub_len, LANES), F32)],
        scratch_shapes=[pltpu.VMEM((3, HEADS_A // 2, 2 * QBLK_A, KWIN_A), F32),
                        pltpu.VMEM((HEADS_A // 2, 2 * QBLK_A, KWIN_A), F32)],
        compiler_params=pltpu.CompilerParams(dimension_semantics=("arbitrary", "arbitrary"),
                                             vmem_limit_bytes=VMEM_LIMIT),
        name=f"dilated{g}",
    )(buckets, table, a_g, a_g, a_g)


def _gqa_kernel(q_ref, k_ref, vt_ref, o_ref, s_ref, p_ref):
    seq = k_ref.shape[0]
    nch = seq // KCH_B
    nqt = seq // TQ_B

    def score_pass(qt, g):
        q0 = pl.multiple_of(qt * TQ_B, TQ_B)
        q = q_ref[pl.ds(q0, TQ_B), g * HD_B:(g + 1) * HD_B]
        mrun = jnp.full((8, TQ_B), NEG_INF, F32)
        for ch in range(nch):
            rows = slice(ch * KCH_B, (ch + 1) * KCH_B)
            st = lax.dot_general(k_ref[rows, :], q, (((1,), (1,)), ((), ())),
                                 preferred_element_type=F32)
            s_ref[g, rows, :] = st
            mrun = jnp.maximum(mrun, jnp.max(st.reshape(KCH_B // 8, 8, TQ_B), axis=0))
        return mrun

    def value_pass(qt, g, mrun):
        q0 = pl.multiple_of(qt * TQ_B, TQ_B)
        m = jnp.max(mrun, axis=0, keepdims=True)
        lrun = jnp.zeros((8, TQ_B), F32)
        for ch in range(nch):
            rows = slice(ch * KCH_B, (ch + 1) * KCH_B)
            p = jnp.exp2(s_ref[g, rows, :] - m)
            lrun = lrun + jnp.sum(p.reshape(KCH_B // 8, 8, TQ_B), axis=0)
            p_ref[g, rows, :] = p.astype(BF16)
        l = jnp.sum(lrun, axis=0, keepdims=True)
        ot = jnp.dot(vt_ref[...], p_ref[g], preferred_element_type=F32)
        o_ref[pl.ds(q0, TQ_B), g * HD_B:(g + 1) * HD_B] = (ot / l).T.astype(BF16)

    def body(qt, mrun):
        for g in range(GQA_B):
            if g + 1 < GQA_B:
                nxt = score_pass(qt, g + 1)
            else:
                nxt = score_pass(jnp.minimum(qt + 1, nqt - 1), 0)
            value_pass(qt, g, mrun)
            mrun = nxt
        return mrun

    lax.fori_loop(0, nqt, body, score_pass(0, 0), unroll=GQA_UNROLL)


def _gqa(qb, kb, vt):
    bsz, seq, _ = qb.shape
    gw = GQA_B * HD_B
    return pl.pallas_call(
        _gqa_kernel,
        grid=(bsz, N_KV_B),
        in_specs=[pl.BlockSpec((None, seq, gw), lambda b, kv: (b, 0, kv)),
                  pl.BlockSpec((None, seq, HD_B), lambda b, kv: (b, 0, kv)),
                  pl.BlockSpec((None, HD_B, seq), lambda b, kv: (b, kv, 0))],
        out_specs=pl.BlockSpec((None, seq, gw), lambda b, kv: (b, 0, kv)),
        out_shape=jax.ShapeDtypeStruct((bsz, seq, N_HEADS_B * HD_B), BF16),
        scratch_shapes=[pltpu.VMEM((GQA_B, seq, TQ_B), F32), pltpu.VMEM((GQA_B, seq, TQ_B), BF16)],
        compiler_params=pltpu.CompilerParams(dimension_semantics=("arbitrary", "arbitrary"),
                                             vmem_limit_bytes=VMEM_LIMIT),
        name="gqa",
    )(qb, kb, vt)


def _merge_kernel(x_ref, gate_ref, ob_ref, o0_ref, s0_ref, o1_ref, s1_ref, o2_ref, s2_ref,
                  wa_ref, wb_ref, wo_ref, expand_ref, out_ref, oslab_ref, sslab_ref, tmp_ref):
    tm = x_ref.shape[0]
    nslab = GW_A // LANES

    def planes(g, src_o, src_s, dst_o, dst_s, lp):
        d = DILATED_GROUPS[g][1]
        d_dst = d // 4
        for r_dst in range(d_dst):
            for q in range(4):
                rows = pl.ds(r_dst * 4 * lp + q, lp, stride=4)
                dst_s(rows, src_s(r_dst + d_dst * q))
                for s in range(nslab):
                    dst_o(s, rows, src_o(r_dst + d_dst * q, s))

    lane_blk = lambda s: slice(s * LANES, (s + 1) * LANES)

    def put_tmp_o(s, rows, val):
        tmp_ref[s, rows, :] = val

    def put_tmp_s(rows, val):
        tmp_ref[nslab, rows, :] = val

    def put_slab_o(g):
        def put(s, rows, val):
            oslab_ref[g, s, rows, :] = val
        return put

    def put_slab_s(g):
        def put(rows, val):
            sslab_ref[g, rows, :] = val
        return put

    l1, l2 = tm // DILATED_GROUPS[1][1], tm // DILATED_GROUPS[2][1]
    planes(1, lambda r, s: o1_ref[r, :, lane_blk(s)].astype(F32), lambda r: s1_ref[r],
           put_slab_o(0), put_slab_s(0), l1)
    planes(2, lambda r, s: o2_ref[r, :, lane_blk(s)].astype(F32), lambda r: s2_ref[r],
           put_tmp_o, put_tmp_s, l2)
    planes(1, lambda r, s: tmp_ref[s, r * l1:(r + 1) * l1, :], lambda r: tmp_ref[nslab, r * l1:(r + 1) * l1, :],
           put_slab_o(1), put_slab_s(1), l1)

    stats = [s0_ref[...], sslab_ref[0], sslab_ref[1]]
    dens = [pltpu.roll(st, LANES - HEADS_A, axis=1) for st in stats]
    mx = jnp.maximum(jnp.maximum(stats[0], stats[1]), stats[2])
    es = [jnp.exp2(st - mx) for st in stats]
    total = es[0] * dens[0] + es[1] * dens[1] + es[2] * dens[2]
    head_lane = lax.broadcasted_iota(jnp.int32, (tm, LANES), 1) < HEADS_A
    o_a = jnp.zeros((tm, GW_A), F32)
    for g in range(N_GROUPS_A):
        w = jnp.where(head_lane, es[g] / total, 0.0)
        w_hi = w.astype(BF16)
        w_lo = (w - w_hi.astype(F32)).astype(BF16)
        wide = jnp.dot(jnp.concatenate([w_hi, w_lo], axis=1), expand_ref[...], preferred_element_type=F32)
        if g == 0:
            acc = o0_ref[...].astype(F32)
        else:
            acc = jnp.concatenate([oslab_ref[g - 1, s] for s in range(nslab)], axis=1)
        o_a = o_a + wide * acc
    o_a = o_a.astype(BF16)

    ya = jnp.dot(o_a, wa_ref[...], preferred_element_type=F32)
    yb = jnp.dot(ob_ref[...], wb_ref[...], preferred_element_type=F32)
    merged = (gate_ref[:, :D_MODEL].astype(F32) * ya + gate_ref[:, D_MODEL:].astype(F32) * yb).astype(BF16)
    out_ref[...] = x_ref[...] + jnp.dot(merged, wo_ref[...], preferred_element_type=F32)


def _merge(x1, gates, ob, outs, stats, wa, wb, wo):
    bsz, seq, _ = x1.shape
    tok = lambda w: pl.BlockSpec((None, TM, w), lambda b, t: (b, t, 0))
    a_specs = []
    for _, d in DILATED_GROUPS:
        for width in (GW_A, LANES):
            if d == 1:
                a_specs.append(pl.BlockSpec((None, None, TM, width), lambda b, t: (b, 0, t, 0)))
            else:
                a_specs.append(pl.BlockSpec((None, d, TM // d, width), lambda b, t: (b, 0, t, 0)))
    a_args = [arr for pair in zip(outs, stats) for arr in pair]
    src = jnp.arange(2 * LANES, dtype=jnp.int32)[:, None] % LANES
    dst = jnp.arange(GW_A, dtype=jnp.int32)[None, :] // HD_A
    expand = (src == dst).astype(BF16)
    return pl.pallas_call(
        _merge_kernel,
        grid=(bsz, seq // TM),
        in_specs=[tok(D_MODEL), tok(2 * D_MODEL), tok(N_HEADS_B * HD_B)] + a_specs
                 + [_resident((GW_A, D_MODEL)), _resident((N_HEADS_B * HD_B, D_MODEL)),
                    _resident((D_MODEL, D_MODEL)), _resident((2 * LANES, GW_A))],
        out_specs=tok(D_MODEL),
        out_shape=jax.ShapeDtypeStruct((bsz, seq, D_MODEL), F32),
        scratch_shapes=[pltpu.VMEM((2, GW_A // LANES, TM, LANES), F32),
                        pltpu.VMEM((2, TM, LANES), F32),
                        pltpu.VMEM((GW_A // LANES + 1, TM, LANES), F32)],
        compiler_params=pltpu.CompilerParams(dimension_semantics=("arbitrary", "arbitrary"),
                                             vmem_limit_bytes=VMEM_LIMIT),
        name="merge",
    )(x1, gates, ob, *a_args, wa, wb, wo, expand)


def _t5_bucket(rel):
    n = N_BUCKETS // 2
    max_exact = n // 2
    ret = jnp.where(rel > 0, n, 0)
    a = jnp.abs(rel)
    af = jnp.maximum(a, 1).astype(F32)
    large = max_exact + (jnp.log(af / max_exact) / math.log(MAX_DISTANCE / max_exact)
                         * (n - max_exact)).astype(jnp.int32)
    large = jnp.minimum(large, n - 1)
    return ret + jnp.where(a < max_exact, a, large)


def _bucket_tiles(dilation, half):
    qi = jnp.arange(QBLK_A, dtype=jnp.int32)[:, None]
    kj = jnp.arange(KWIN_A, dtype=jnp.int32)[None, :]
    tiles = []
    for shift in (0, (KWIN_A - QBLK_A) // 2, KWIN_A - QBLK_A):
        rel = kj - shift - qi
        tiles.append(jnp.where(jnp.abs(rel) <= half, _t5_bucket(rel * dilation), -1))
    return jnp.stack(tiles)


def _rope_tables(seq):
    t = np.arange(seq)
    n_freq = HD_B // 4
    freq = (ROPE_THETA ** (-np.arange(n_freq, dtype=np.float32) / n_freq)).astype(np.float32)
    ang = np.concatenate([(t // GRID_W).astype(np.float32)[:, None] * freq,
                          (t % GRID_W).astype(np.float32)[:, None] * freq], -1)
    cos, sin = np.cos(ang), np.sin(ang)
    cosf = np.repeat(cos, 2, axis=-1)
    sinf = np.stack([-sin, sin], axis=-1).reshape(seq, HD_B)
    return jnp.asarray(cosf, F32), jnp.asarray(sinf, F32)


def kernel(x, ffn1_norm, ffn1_w1, ffn1_w3, ffn1_w2, mix_norm, w_in, b_gate, q_norm, k_norm, rel_bias,
           w_branch_a, w_branch_b, w_out, ffn2_norm, ffn2_w1, ffn2_w3, ffn2_w2, final_norm):
    bsz, seq, dm = x.shape
    assert dm == D_MODEL and seq % TM == 0 and ffn1_w1.shape[0] == 1
    row = lambda v: v.reshape(1, -1).astype(F32)
    cosf, sinf = _rope_tables(seq)
    qn = row(q_norm[0] * (HD_B ** -0.5 * LOG2E))
    kn = row(k_norm[0])

    x1 = _ffn(x.reshape(bsz * seq, dm), row(ffn1_norm[0]), ffn1_w1[0].astype(BF16), ffn1_w3[0].astype(BF16),
              ffn1_w2[0].astype(BF16), row(final_norm), False).reshape(bsz, seq, dm)
    a0, a1, a2, qb, kb, vt, gates = _proj(x1, row(mix_norm[0]), w_in[0].astype(BF16), row(b_gate[0]), qn, kn,
                                          cosf, sinf)

    outs, stats = [], []
    for g, (a_g, (window, d)) in enumerate(zip((a0, a1, a2), DILATED_GROUPS)):
        table = rel_bias[:, g * HEADS_A:(g + 1) * HEADS_A].astype(F32)
        o_g, st_g = _dilated(a_g, _bucket_tiles(d, window // (2 * d)), table, g)
        outs.append(o_g)
        stats.append(st_g)
    ob = _gqa(qb, kb, vt)

    x2 = _merge(x1, gates, ob, outs, stats, w_branch_a[0].astype(BF16), w_branch_b[0].astype(BF16),
                w_out[0].astype(BF16))
    y = _ffn(x2.reshape(bsz * seq, dm), row(ffn2_norm[0]), ffn2_w1[0].astype(BF16), ffn2_w3[0].astype(BF16),
             ffn2_w2[0].astype(BF16), row(final_norm), True)
    return y.reshape(bsz, seq, dm)
```

```python
import functools
import math

import jax
import jax.numpy as jnp
import numpy as np
from jax import lax
from jax.experimental import pallas as pl
from jax.experimental.pallas import tpu as pltpu

D_MODEL = 1024
D_FF = 2816
DILATED_GROUPS = ((128, 1), (512, 4), (2048, 16))
N_GROUPS_A = 3
HEADS_A = 8
HD_A = 64
GW_A = HEADS_A * HD_A
N_HEADS_B = 8
N_KV_B = 2
GQA_B = N_HEADS_B // N_KV_B
HD_B = 128
GRID_W = 64
ROPE_THETA = 10000.0
N_BUCKETS = 32
MAX_DISTANCE = 1024
EPS = 1e-6
NEG_INF = -1e30
LOG2E = 1.4426950408889634

LANES = 128
TM = 512
TM_FFN = 1024
FFN_CHUNK = 256
PROJ_CHUNK = 256
QBLK_A = 128
KWIN_A = 256
DILATED_ROWS = 4096
DILATED_UNROLL = 8
TQ_B = 256
KCH_B = 512
GQA_UNROLL = 2
VMEM_LIMIT = 60 * 1024 * 1024
WEIGHT_STAGE_BYTES = 2 * 1024 * 1024

BF16 = jnp.bfloat16
F32 = jnp.float32


_HBM = pl.BlockSpec(memory_space=pl.ANY)


def _resident(shape):
    nd = len(shape)
    return pl.BlockSpec(shape, lambda *_: (0,) * nd, pipeline_mode=pl.Buffered(1))


def _rms(x, gain):
    ms = jnp.mean(x * x, axis=-1, keepdims=True)
    return x * lax.rsqrt(ms + EPS) * gain


def _load_weight_bf16(src_hbm, dst_ref):
    rows, cols = src_hbm.shape
    chunk_rows = max(r for r in range(16, rows + 1, 16)
                     if rows % r == 0 and r * cols * 4 <= WEIGHT_STAGE_BYTES)
    nchunk = rows // chunk_rows

    def run(stage_ref, sem_ref):
        def copy(i):
            return pltpu.make_async_copy(src_hbm.at[pl.ds(i * chunk_rows, chunk_rows), :],
                                         stage_ref.at[i % 2], sem_ref.at[i % 2])
        copy(0).start()
        for i in range(nchunk):
            if i + 1 < nchunk:
                copy(i + 1).start()
            copy(i).wait()
            dst_ref[i * chunk_rows:(i + 1) * chunk_rows, :] = stage_ref[i % 2].astype(BF16)

    pl.run_scoped(run, pltpu.VMEM((2, chunk_rows, cols), F32), pltpu.SemaphoreType.DMA((2,)))


def _ffn_kernel(x_ref, g_ref, w1_hbm, w3_hbm, w2_hbm, fg_ref, o_ref, w1_ref, w3_ref, w2_ref, *, final_norm):
    @pl.when(pl.program_id(0) == 0)
    def _load_weights():
        _load_weight_bf16(w1_hbm, w1_ref)
        _load_weight_bf16(w3_hbm, w3_ref)
        _load_weight_bf16(w2_hbm, w2_ref)

    x = x_ref[...]
    h = _rms(x, g_ref[...]).astype(BF16)
    acc = jnp.zeros((x.shape[0], D_MODEL), F32)
    for j in range(D_FF // FFN_CHUNK):
        c = slice(j * FFN_CHUNK, (j + 1) * FFN_CHUNK)
        a = jnp.dot(h, w1_ref[:, c], preferred_element_type=F32)
        b = jnp.dot(h, w3_ref[:, c], preferred_element_type=F32)
        g = (a * jax.nn.sigmoid(a) * b).astype(BF16)
        acc = acc + jnp.dot(g, w2_ref[c, :], preferred_element_type=F32)
    y = x + 0.5 * acc
    if final_norm:
        y = _rms(y, fg_ref[...])
    o_ref[...] = y


def _ffn(x2d, gain, w1, w3, w2, final_gain, final_norm):
    n = x2d.shape[0]
    assert n % TM_FFN == 0
    row = pl.BlockSpec((TM_FFN, D_MODEL), lambda i: (i, 0))
    return pl.pallas_call(
        functools.partial(_ffn_kernel, final_norm=final_norm),
        grid=(n // TM_FFN,),
        in_specs=[row, _resident((1, D_MODEL)), _HBM, _HBM, _HBM, _resident((1, D_MODEL))],
        out_specs=row,
        out_shape=jax.ShapeDtypeStruct((n, D_MODEL), F32),
        scratch_shapes=[pltpu.VMEM((D_MODEL, D_FF), BF16), pltpu.VMEM((D_MODEL, D_FF), BF16),
                        pltpu.VMEM((D_FF, D_MODEL), BF16)],
        compiler_params=pltpu.CompilerParams(dimension_semantics=("arbitrary",), vmem_limit_bytes=VMEM_LIMIT),
        name="ffn_final" if final_norm else "ffn",
    )(x2d, gain, w1, w3, w2, final_gain)


A_WIDTH = 3 * GW_A
QSCALE_A = HD_A ** -0.5 * LOG2E
OFF_QB = N_GROUPS_A * A_WIDTH
OFF_KB = OFF_QB + N_HEADS_B * HD_B
OFF_VB = OFF_KB + N_KV_B * HD_B
OFF_GATE = OFF_VB + N_KV_B * HD_B
IN_WIDTH = OFF_GATE + 2 * D_MODEL


def _proj_kernel(x_ref, g_ref, w_hbm, bg_ref, qn_ref, kn_ref, cos_ref, sin_ref,
                 a0_ref, a1_ref, a2_ref, qb_ref, kb_ref, vt_ref, gate_ref, hs_ref, hb_ref, w_ref):
    @pl.when((pl.program_id(0) == 0) & (pl.program_id(1) == 0))
    def _load_weights():
        _load_weight_bf16(w_hbm, w_ref)

    tm = x_ref.shape[0]
    nslab = D_MODEL // LANES
    even_lane = (lax.broadcasted_iota(jnp.int32, (tm, HD_B), 1) & 1) == 0

    hn = _rms(x_ref[...], g_ref[...])
    hb_ref[0] = hn.astype(BF16)
    for s in range(nslab):
        hs_ref[0, s] = hn[:, s * LANES:(s + 1) * LANES]
    d_prev = 1
    for g in range(1, N_GROUPS_A):
        d = DILATED_GROUPS[g][1]
        f, lp, ln = d // d_prev, tm // d_prev, tm // d
        for r_prev in range(d_prev):
            for q in range(f):
                r = r_prev + d_prev * q
                for s in range(nslab):
                    rows = hs_ref[(g - 1) % 2, s, pl.ds(r_prev * lp + q, ln, stride=f), :]
                    hb_ref[g, r * ln:(r + 1) * ln, s * LANES:(s + 1) * LANES] = rows.astype(BF16)
                    if g + 1 < N_GROUPS_A:
                        hs_ref[g % 2, s, r * ln:(r + 1) * ln, :] = rows
        d_prev = d

    def mm(lo, g=0):
        return jnp.dot(hb_ref[g], w_ref[:, lo:lo + PROJ_CHUNK], preferred_element_type=F32)

    def _rope(y, cosf, sinf):
        partner = jnp.where(even_lane, pltpu.roll(y, HD_B - 1, axis=1), pltpu.roll(y, 1, axis=1))
        return y * cosf + partner * sinf

    def a_chunk(g, j):
        part, half = divmod(j, GW_A // PROJ_CHUNK)
        p = mm((part * N_GROUPS_A + g) * GW_A + half * PROJ_CHUNK, g)
        return p * QSCALE_A if part == 0 else p

    for j in range(2 * D_MODEL // PROJ_CHUNK):
        c = slice(j * PROJ_CHUNK, (j + 1) * PROJ_CHUNK)
        pg = mm(OFF_GATE + j * PROJ_CHUNK) + bg_ref[:, c]
        gate_ref[:, c] = (0.5 * jnp.tanh(0.5 * pg) + 0.5).astype(BF16)

    cosf = cos_ref[...]
    sinf = sin_ref[...]
    heads_per_chunk = PROJ_CHUNK // HD_B
    for j in range(N_HEADS_B // heads_per_chunk):
        pq = mm(OFF_QB + j * PROJ_CHUNK)
        for hh in range(heads_per_chunk):
            src = slice(hh * HD_B, (hh + 1) * HD_B)
            dst = slice(j * PROJ_CHUNK + hh * HD_B, j * PROJ_CHUNK + (hh + 1) * HD_B)
            qb_ref[:, dst] = _rope(_rms(pq[:, src], qn_ref[...]), cosf, sinf).astype(BF16)
    pk = mm(OFF_KB)
    for hh in range(N_KV_B):
        c = slice(hh * HD_B, (hh + 1) * HD_B)
        kb_ref[:, c] = _rope(_rms(pk[:, c], kn_ref[...]), cosf, sinf).astype(BF16)
    vt_ref[...] = mm(OFF_VB).T.astype(BF16)

    for g, a_ref in ((2, a2_ref), (1, a1_ref)):
        d = DILATED_GROUPS[g][1]
        for j in range(A_WIDTH // PROJ_CHUNK):
            c = slice(j * PROJ_CHUNK, (j + 1) * PROJ_CHUNK)
            a_ref[:, :, c] = a_chunk(g, j).reshape(d, tm // d, PROJ_CHUNK).astype(BF16)
    for j in range(A_WIDTH // PROJ_CHUNK):
        c = slice(j * PROJ_CHUNK, (j + 1) * PROJ_CHUNK)
        a0_ref[:, c] = a_chunk(0, j).astype(BF16)


def _proj(x1, gain, w_in, b_gate, qn, kn, cosf, sinf):
    bsz, seq, _ = x1.shape
    nt = seq // TM
    outs, specs = [], []
    for _, d in DILATED_GROUPS:
        outs.append(jax.ShapeDtypeStruct((bsz, d, seq // d, A_WIDTH), BF16))
        specs.append(pl.BlockSpec((None, d, TM // d, A_WIDTH), lambda b, t: (b, 0, t, 0)))
    outs += [jax.ShapeDtypeStruct((bsz, seq, N_HEADS_B * HD_B), BF16),
             jax.ShapeDtypeStruct((bsz, seq, N_KV_B * HD_B), BF16),
             jax.ShapeDtypeStruct((bsz, N_KV_B * HD_B, seq), BF16),
             jax.ShapeDtypeStruct((bsz, seq, 2 * D_MODEL), BF16)]
    specs += [pl.BlockSpec((None, TM, N_HEADS_B * HD_B), lambda b, t: (b, t, 0)),
              pl.BlockSpec((None, TM, N_KV_B * HD_B), lambda b, t: (b, t, 0)),
              pl.BlockSpec((None, N_KV_B * HD_B, TM), lambda b, t: (b, 0, t)),
              pl.BlockSpec((None, TM, 2 * D_MODEL), lambda b, t: (b, t, 0))]
    specs[0] = pl.BlockSpec((None, None, TM, A_WIDTH), lambda b, t: (b, 0, t, 0))
    return pl.pallas_call(
        _proj_kernel,
        grid=(bsz, nt),
        in_specs=[pl.BlockSpec((None, TM, D_MODEL), lambda b, t: (b, t, 0)),
                  _resident((1, D_MODEL)), _HBM, _resident((1, 2 * D_MODEL)),
                  _resident((1, HD_B)), _resident((1, HD_B)),
                  pl.BlockSpec((TM, HD_B), lambda b, t: (t, 0)),
                  pl.BlockSpec((TM, HD_B), lambda b, t: (t, 0))],
        out_specs=specs,
        out_shape=outs,
        scratch_shapes=[pltpu.VMEM((2, D_MODEL // LANES, TM, LANES), F32),
                        pltpu.VMEM((N_GROUPS_A, TM, D_MODEL), BF16),
                        pltpu.VMEM((D_MODEL, IN_WIDTH), BF16)],
        compiler_params=pltpu.CompilerParams(dimension_semantics=("arbitrary", "arbitrary"),
                                             vmem_limit_bytes=VMEM_LIMIT),
        name="proj",
    )(x1, gain, w_in, b_gate, qn, kn, cosf, sinf)


def _dilated_kernel(bucket_ref, tab_ref, q_ref, k_ref, v_ref, o_ref, st_ref, bias_ref, s_ref, *, sub_len):
    nblk = sub_len // QBLK_A
    nunit = q_ref.shape[0] * nblk
    npair = HEADS_A // 2

    @pl.when((pl.program_id(0) == 0) & (pl.program_id(1) == 0))
    def _build_bias():
        for v in range(3):
            bt = bucket_ref[v]
            accs = [jnp.full((QBLK_A, KWIN_A), NEG_INF, F32)] * HEADS_A
            for j in range(N_BUCKETS):
                hit = bt == j
                accs = [jnp.where(hit, tab_ref[j, hh] * LOG2E, acc) for hh, acc in enumerate(accs)]
            for hh, acc in enumerate(accs):
                bias_ref[v, hh // 2, (hh % 2) * QBLK_A:(hh % 2 + 1) * QBLK_A, :] = acc

    lane = lax.broadcasted_iota(jnp.int32, (QBLK_A, LANES), 1)
    low = lane < HD_A

    def geometry(u):
        r = u >> (nblk.bit_length() - 1)
        n = u & (nblk - 1)
        q0 = pl.multiple_of(n * QBLK_A, QBLK_A)
        w0 = pl.multiple_of(jnp.clip(n * QBLK_A - (KWIN_A - QBLK_A) // 2, 0, sub_len - KWIN_A), 64)
        variant = jnp.where(n == 0, 0, jnp.where(n == nblk - 1, 2, 1))
        return r, q0, w0, variant

    def score_pass(u, pair):
        r, q0, w0, variant = geometry(u)
        c = slice(pair * LANES, (pair + 1) * LANES)
        qp = q_ref[r, pl.ds(q0, QBLK_A), c]
        kp = k_ref[r, pl.ds(w0, KWIN_A), c]
        zero = jnp.zeros_like(qp)
        q2 = jnp.concatenate([jnp.where(low, qp, zero), jnp.where(low, zero, qp)], axis=0)
        s = lax.dot_general(q2, kp, (((1,), (1,)), ((), ())), preferred_element_type=F32)
        s = s + bias_ref[variant, pair]
        s_ref[pair] = s
        return jnp.max(s, axis=-1, keepdims=True)

    def value_pass(u, pair, m, stats):
        r, q0, w0, _ = geometry(u)
        c = slice(pair * LANES, (pair + 1) * LANES)
        vp = v_ref[r, pl.ds(w0, KWIN_A), c]
        p = jnp.exp2(s_ref[pair] - m)
        l = jnp.sum(p, axis=-1, keepdims=True)
        o = jnp.dot(p.astype(BF16), vp, preferred_element_type=F32)
        o_ref[r, pl.ds(q0, QBLK_A), c] = jnp.where(low, o[:QBLK_A], o[QBLK_A:]).astype(BF16)
        for half in range(2):
            rows = slice(half * QBLK_A, (half + 1) * QBLK_A)
            hh = 2 * pair + half
            stats = jnp.where(lane == hh, m[rows], stats)
            stats = jnp.where(lane == HEADS_A + hh, l[rows], stats)
        return stats

    def body(u, ms):
        stats = jnp.zeros((QBLK_A, LANES), F32)
        for pair in range(npair):
            if pair + 1 < npair:
                nxt = score_pass(u, pair + 1)
            else:
                nxt = score_pass(jnp.minimum(u + 1, nunit - 1), 0)
            stats = value_pass(u, pair, ms, stats)
            ms = nxt
        r, q0, _, _ = geometry(u)
        st_ref[r, pl.ds(q0, QBLK_A), :] = stats
        return ms

    lax.fori_loop(0, nunit, body, score_pass(0, 0), unroll=DILATED_UNROLL)


def _dilated(a_g, buckets, table, g):
    bsz, d, sub_len, _ = a_g.shape
    nblk = sub_len // QBLK_A
    assert sub_len % QBLK_A == 0 and sub_len >= KWIN_A and nblk & (nblk - 1) == 0
    rb = min(d, max(1, DILATED_ROWS // sub_len))
    assert d % rb == 0
    blk = lambda col: pl.BlockSpec((None, rb, sub_len, GW_A), lambda b, r: (b, r, 0, col))
    return pl.pallas_call(
        functools.partial(_dilated_kernel, sub_len=sub_len),
        grid=(bsz, d // rb),
        in_specs=[_resident((3, QBLK_A, KWIN_A)),
                  pl.BlockSpec(memory_space=pltpu.SMEM),
                  blk(0), blk(1), blk(2)],
        out_specs=[blk(0), pl.BlockSpec((None, rb, sub_len, LANES), lambda b, r: (b, r, 0, 0))],
        out_shape=[jax.ShapeDtypeStruct((bsz, d, sub_len, GW_A), BF16),
                   jax.ShapeDtypeStruct((bsz, d, sub_len, LANES), F32)],
        scratch_shapes=[pltpu.VMEM((3, HEADS_A // 2, 2 * QBLK_A, KWIN_A), F32),
                        pltpu.VMEM((HEADS_A // 2, 2 * QBLK_A, KWIN_A), F32)],
        compiler_params=pltpu.CompilerParams(dimension_semantics=("arbitrary", "arbitrary"),
                                             vmem_limit_bytes=VMEM_LIMIT),
        name=f"dilated{g}",
    )(buckets, table, a_g, a_g, a_g)


def _gqa_kernel(q_ref, k_ref, vt_ref, o_ref, s_ref, p_ref):
    seq = k_ref.shape[0]
    nch = seq // KCH_B
    nqt = seq // TQ_B

    def score_pass(qt, g):
        q0 = pl.multiple_of(qt * TQ_B, TQ_B)
        q = q_ref[pl.ds(q0, TQ_B), g * HD_B:(g + 1) * HD_B]
        mrun = jnp.full((8, TQ_B), NEG_INF, F32)
        for ch in range(nch):
            rows = slice(ch * KCH_B, (ch + 1) * KCH_B)
            st = lax.dot_general(k_ref[rows, :], q, (((1,), (1,)), ((), ())),
                                 preferred_element_type=F32)
            s_ref[g, rows, :] = st
            mrun = jnp.maximum(mrun, jnp.max(st.reshape(KCH_B // 8, 8, TQ_B), axis=0))
        return mrun

    def value_pass(qt, g, mrun):
        q0 = pl.multiple_of(qt * TQ_B, TQ_B)
        m = jnp.max(mrun, axis=0, keepdims=True)
        lrun = jnp.zeros((8, TQ_B), F32)
        for ch in range(nch):
            rows = slice(ch * KCH_B, (ch + 1) * KCH_B)
            p = jnp.exp2(s_ref[g, rows, :] - m)
            lrun = lrun + jnp.sum(p.reshape(KCH_B // 8, 8, TQ_B), axis=0)
            p_ref[g, rows, :] = p.astype(BF16)
        l = jnp.sum(lrun, axis=0, keepdims=True)
        ot = jnp.dot(vt_ref[...], p_ref[g], preferred_element_type=F32)
        o_ref[pl.ds(q0, TQ_B), g * HD_B:(g + 1) * HD_B] = (ot / l).T.astype(BF16)

    def body(qt, mrun):
        for g in range(GQA_B):
            if g + 1 < GQA_B:
                nxt = score_pass(qt, g + 1)
            else:
                nxt = score_pass(jnp.minimum(qt + 1, nqt - 1), 0)
            value_pass(qt, g, mrun)
            mrun = nxt
        return mrun

    lax.fori_loop(0, nqt, body, score_pass(0, 0), unroll=GQA_UNROLL)


def _gqa(qb, kb, vt):
    bsz, seq, _ = qb.shape
    gw = GQA_B * HD_B
    return pl.pallas_call(
        _gqa_kernel,
        grid=(bsz, N_KV_B),
        in_specs=[pl.BlockSpec((None, seq, gw), lambda b, kv: (b, 0, kv)),
                  pl.BlockSpec((None, seq, HD_B), lambda b, kv: (b, 0, kv)),
                  pl.BlockSpec((None, HD_B, seq), lambda b, kv: (b, kv, 0))],
        out_specs=pl.BlockSpec((None, seq, gw), lambda b, kv: (b, 0, kv)),
        out_shape=jax.ShapeDtypeStruct((bsz, seq, N_HEADS_B * HD_B), BF16),
        scratch_shapes=[pltpu.VMEM((GQA_B, seq, TQ_B), F32), pltpu.VMEM((GQA_B, seq, TQ_B), BF16)],
        compiler_params=pltpu.CompilerParams(dimension_semantics=("arbitrary", "arbitrary"),
                                             vmem_limit_bytes=VMEM_LIMIT),
        name="gqa",
    )(qb, kb, vt)


def _merge_kernel(x_ref, gate_ref, ob_ref, o0_ref, s0_ref, o1_ref, s1_ref, o2_ref, s2_ref,
                  wa_ref, wb_ref, wo_ref, expand_ref, out_ref, oslab_ref, sslab_ref, tmp_ref):
    tm = x_ref.shape[0]
    nslab = GW_A // LANES

    def planes(g, src_o, src_s, dst_o, dst_s, lp):
        d = DILATED_GROUPS[g][1]
        d_dst = d // 4
        for r_dst in range(d_dst):
            for q in range(4):
                rows = pl.ds(r_dst * 4 * lp + q, lp, stride=4)
                dst_s(rows, src_s(r_dst + d_dst * q))
                for s in range(nslab):
                    dst_o(s, rows, src_o(r_dst + d_dst * q, s))

    lane_blk = lambda s: slice(s * LANES, (s + 1) * LANES)

    def put_tmp_o(s, rows, val):
        tmp_ref[s, rows, :] = val

    def put_tmp_s(rows, val):
        tmp_ref[nslab, rows, :] = val

    def put_slab_o(g):
        def put(s, rows, val):
            oslab_ref[g, s, rows, :] = val
        return put

    def put_slab_s(g):
        def put(rows, val):
            sslab_ref[g, rows, :] = val
        return put

    l1, l2 = tm // DILATED_GROUPS[1][1], tm // DILATED_GROUPS[2][1]
    planes(1, lambda r, s: o1_ref[r, :, lane_blk(s)].astype(F32), lambda r: s1_ref[r],
           put_slab_o(0), put_slab_s(0), l1)
    planes(2, lambda r, s: o2_ref[r, :, lane_blk(s)].astype(F32), lambda r: s2_ref[r],
           put_tmp_o, put_tmp_s, l2)
    planes(1, lambda r, s: tmp_ref[s, r * l1:(r + 1) * l1, :], lambda r: tmp_ref[nslab, r * l1:(r + 1) * l1, :],
           put_slab_o(1), put_slab_s(1), l1)

    stats = [s0_ref[...], sslab_ref[0], sslab_ref[1]]
    dens = [pltpu.roll(st, LANES - HEADS_A, axis=1) for st in stats]
    mx = jnp.maximum(jnp.maximum(stats[0], stats[1]), stats[2])
    es = [jnp.exp2(st - mx) for st in stats]
    total = es[0] * dens[0] + es[1] * dens[1] + es[2] * dens[2]
    head_lane = lax.broadcasted_iota(jnp.int32, (tm, LANES), 1) < HEADS_A
    o_a = jnp.zeros((tm, GW_A), F32)
    for g in range(N_GROUPS_A):
        w = jnp.where(head_lane, es[g] / total, 0.0)
        w_hi = w.astype(BF16)
        w_lo = (w - w_hi.astype(F32)).astype(BF16)
        wide = jnp.dot(jnp.concatenate([w_hi, w_lo], axis=1), expand_ref[...], preferred_element_type=F32)
        if g == 0:
            acc = o0_ref[...].astype(F32)
        else:
            acc = jnp.concatenate([oslab_ref[g - 1, s] for s in range(nslab)], axis=1)
        o_a = o_a + wide * acc
    o_a = o_a.astype(BF16)

    ya = jnp.dot(o_a, wa_ref[...], preferred_element_type=F32)
    yb = jnp.dot(ob_ref[...], wb_ref[...], preferred_element_type=F32)
    merged = (gate_ref[:, :D_MODEL].astype(F32) * ya + gate_ref[:, D_MODEL:].astype(F32) * yb).astype(BF16)
    out_ref[...] = x_ref[...] + jnp.dot(merged, wo_ref[...], preferred_element_type=F32)


def _merge(x1, gates, ob, outs, stats, wa, wb, wo):
    bsz, seq, _ = x1.shape
    tok = lambda w: pl.BlockSpec((None, TM, w), lambda b, t: (b, t, 0))
    a_specs = []
    for _, d in DILATED_GROUPS:
        for width in (GW_A, LANES):
            if d == 1:
                a_specs.append(pl.BlockSpec((None, None, TM, width), lambda b, t: (b, 0, t, 0)))
            else:
                a_specs.append(pl.BlockSpec((None, d, TM // d, width), lambda b, t: (b, 0, t, 0)))
    a_args = [arr for pair in zip(outs, stats) for arr in pair]
    src = jnp.arange(2 * LANES, dtype=jnp.int32)[:, None] % LANES
    dst = jnp.arange(GW_A, dtype=jnp.int32)[None, :] // HD_A
    expand = (src == dst).astype(BF16)
    return pl.pallas_call(
        _merge_kernel,
        grid=(bsz, seq // TM),
        in_specs=[tok(D_MODEL), tok(2 * D_MODEL), tok(N_HEADS_B * HD_B)] + a_specs
                 + [_resident((GW_A, D_MODEL)), _resident((N_HEADS_B * HD_B, D_MODEL)),
                    _resident((D_MODEL, D_MODEL)), _resident((2 * LANES, GW_A))],
        out_specs=tok(D_MODEL),
        out_shape=jax.ShapeDtypeStruct((bsz, seq, D_MODEL), F32),
        scratch_shapes=[pltpu.VMEM((2, GW_A // LANES, TM, LANES), F32),
                        pltpu.VMEM((2, TM, LANES), F32),
                        pltpu.VMEM((GW_A // LANES + 1, TM, LANES), F32)],
        compiler_params=pltpu.CompilerParams(dimension_semantics=("arbitrary", "arbitrary"),
                                             vmem_limit_bytes=VMEM_LIMIT),
        name="merge",
    )(x1, gates, ob, *a_args, wa, wb, wo, expand)


def _t5_bucket(rel):
    n = N_BUCKETS // 2
    max_exact = n // 2
    ret = jnp.where(rel > 0, n, 0)
    a = jnp.abs(rel)
    af = jnp.maximum(a, 1).astype(F32)
    large = max_exact + (jnp.log(af / max_exact) / math.log(MAX_DISTANCE / max_exact)
                         * (n - max_exact)).astype(jnp.int32)
    large = jnp.minimum(large, n - 1)
    return ret + jnp.where(a < max_exact, a, large)


def _bucket_tiles(dilation, half):
    qi = jnp.arange(QBLK_A, dtype=jnp.int32)[:, None]
    kj = jnp.arange(KWIN_A, dtype=jnp.int32)[None, :]
    tiles = []
    for shift in (0, (KWIN_A - QBLK_A) // 2, KWIN_A - QBLK_A):
        rel = kj - shift - qi
        tiles.append(jnp.where(jnp.abs(rel) <= half, _t5_bucket(rel * dilation), -1))
    return jnp.stack(tiles)


def _rope_tables(seq):
    t = np.arange(seq)
    n_freq = HD_B // 4
    freq = (ROPE_THETA ** (-np.arange(n_freq, dtype=np.float32) / n_freq)).astype(np.float32)
    ang = np.concatenate([(t // GRID_W).astype(np.float32)[:, None] * freq,
                          (t % GRID_W).astype(np.float32)[:, None] * freq], -1)
    cos, sin = np.cos(ang), np.sin(ang)
    cosf = np.repeat(cos, 2, axis=-1)
    sinf = np.stack([-sin, sin], axis=-1).reshape(seq, HD_B)
    return jnp.asarray(cosf, F32), jnp.asarray(sinf, F32)


def kernel(x, ffn1_norm, ffn1_w1, ffn1_w3, ffn1_w2, mix_norm, w_in, b_gate, q_norm, k_norm, rel_bias,
           w_branch_a, w_branch_b, w_out, ffn2_norm, ffn2_w1, ffn2_w3, ffn2_w2, final_norm):
    bsz, seq, dm = x.shape
    assert dm == D_MODEL and seq % TM == 0 and ffn1_w1.shape[0] == 1
    row = lambda v: v.reshape(1, -1).astype(F32)
    cosf, sinf = _rope_tables(seq)
    qn = row(q_norm[0] * (HD_B ** -0.5 * LOG2E))
    kn = row(k_norm[0])

    x1 = _ffn(x.reshape(bsz * seq, dm), row(ffn1_norm[0]), ffn1_w1[0], ffn1_w3[0], ffn1_w2[0],
              row(final_norm), False).reshape(bsz, seq, dm)
    a0, a1, a2, qb, kb, vt, gates = _proj(x1, row(mix_norm[0]), w_in[0], row(b_gate[0]), qn, kn, cosf, sinf)

    outs, stats = [], []
    for g, (a_g, (window, d)) in enumerate(zip((a0, a1, a2), DILATED_GROUPS)):
        table = rel_bias[:, g * HEADS_A:(g + 1) * HEADS_A].astype(F32)
        o_g, st_g = _dilated(a_g, _bucket_tiles(d, window // (2 * d)), table, g)
        outs.append(o_g)
        stats.append(st_g)
    ob = _gqa(qb, kb, vt)

    x2 = _merge(x1, gates, ob, outs, stats, w_branch_a[0].astype(BF16), w_branch_b[0].astype(BF16),
                w_out[0].astype(BF16))
    y = _ffn(x2.reshape(bsz * seq, dm), row(ffn2_norm[0]), ffn2_w1[0], ffn2_w3[0], ffn2_w2[0],
             row(final_norm), True)
    return y.reshape(bsz, seq, dm)
```

```python
import functools
import math

import jax
import jax.numpy as jnp
import numpy as np
from jax import lax
from jax.experimental import pallas as pl
from jax.experimental.pallas import tpu as pltpu

D_MODEL = 1024
D_FF = 2816
DILATED_GROUPS = ((128, 1), (512, 4), (2048, 16))
N_GROUPS_A = 3
HEADS_A = 8
HD_A = 64
GW_A = HEADS_A * HD_A
N_HEADS_B = 8
N_KV_B = 2
GQA_B = N_HEADS_B // N_KV_B
HD_B = 128
GRID_W = 64
ROPE_THETA = 10000.0
N_BUCKETS = 32
MAX_DISTANCE = 1024
EPS = 1e-6
NEG_INF = -1e30
LOG2E = 1.4426950408889634

LANES = 128
TM = 512
TM_FFN = 1024
FFN_CHUNK = 256
PROJ_CHUNK = 256
QBLK_A = 128
KWIN_A = 256
DILATED_ROWS = 4096
DILATED_UNROLL = 8
TQ_B = 256
KCH_B = 512
GQA_UNROLL = 4
VMEM_LIMIT = 60 * 1024 * 1024
WEIGHT_STAGE_BYTES = 2 * 1024 * 1024
WEIGHT_STAGE_SLOTS = 4

BF16 = jnp.bfloat16
F32 = jnp.float32


_HBM = pl.BlockSpec(memory_space=pl.ANY)


def _resident(shape):
    nd = len(shape)
    return pl.BlockSpec(shape, lambda *_: (0,) * nd, pipeline_mode=pl.Buffered(1))


def _rms(x, gain):
    ms = jnp.mean(x * x, axis=-1, keepdims=True)
    return x * lax.rsqrt(ms + EPS) * gain


def _load_weight_bf16(src_hbm, dst_ref):
    rows, cols = src_hbm.shape
    chunk_rows = max(r for r in range(16, rows + 1, 16)
                     if rows % r == 0 and r * cols * 4 <= WEIGHT_STAGE_BYTES)
    nchunk = rows // chunk_rows

    ahead = WEIGHT_STAGE_SLOTS - 1

    def run(stage_ref, sem_ref):
        def copy(i):
            slot = i % WEIGHT_STAGE_SLOTS
            return pltpu.make_async_copy(src_hbm.at[pl.ds(i * chunk_rows, chunk_rows), :],
                                         stage_ref.at[slot], sem_ref.at[slot])
        for i in range(min(ahead, nchunk)):
            copy(i).start()
        for i in range(nchunk):
            if i + ahead < nchunk:
                copy(i + ahead).start()
            copy(i).wait()
            dst_ref[i * chunk_rows:(i + 1) * chunk_rows, :] = stage_ref[i % WEIGHT_STAGE_SLOTS].astype(BF16)

    pl.run_scoped(run, pltpu.VMEM((WEIGHT_STAGE_SLOTS, chunk_rows, cols), F32),
                  pltpu.SemaphoreType.DMA((WEIGHT_STAGE_SLOTS,)))


def _side_cast_specs(weights, nsteps, step_of):
    in_specs, out_specs, out_shapes = [], [], []
    for w in weights:
        rows, cols = w.shape
        assert rows % (16 * nsteps) == 0, (w.shape, nsteps)
        index = lambda *ids: (step_of(*ids), 0)
        in_specs.append(pl.BlockSpec((rows // nsteps, cols), index))
        out_specs.append(pl.BlockSpec((rows // nsteps, cols), index))
        out_shapes.append(jax.ShapeDtypeStruct((rows, cols), BF16))
    return in_specs, out_specs, out_shapes


def _side_casts(side_in, side_out):
    for src, dst in zip(side_in, side_out, strict=True):
        dst[...] = src[...].astype(BF16)


def _ffn_kernel(x_ref, g_ref, w1_in, w3_in, w2_in, fg_ref, *rest, final_norm, n_side):
    side_in, (o_ref, *side_out), scratch = rest[:n_side], rest[n_side:2 * n_side + 1], rest[2 * n_side + 1:]
    _side_casts(side_in, side_out)
    if scratch:
        w1_ref, w3_ref, w2_ref = scratch

        @pl.when(pl.program_id(0) == 0)
        def _load_weights():
            _load_weight_bf16(w1_in, w1_ref)
            _load_weight_bf16(w3_in, w3_ref)
            _load_weight_bf16(w2_in, w2_ref)
    else:
        w1_ref, w3_ref, w2_ref = w1_in, w3_in, w2_in

    x = x_ref[...]
    h = _rms(x, g_ref[...]).astype(BF16)
    acc = jnp.zeros((x.shape[0], D_MODEL), F32)
    for j in range(D_FF // FFN_CHUNK):
        c = slice(j * FFN_CHUNK, (j + 1) * FFN_CHUNK)
        a = jnp.dot(h, w1_ref[:, c], preferred_element_type=F32)
        b = jnp.dot(h, w3_ref[:, c], preferred_element_type=F32)
        g = (a * jax.nn.sigmoid(a) * b).astype(BF16)
        acc = acc + jnp.dot(g, w2_ref[c, :], preferred_element_type=F32)
    y = x + 0.5 * acc
    if final_norm:
        y = _rms(y, fg_ref[...])
    o_ref[...] = y


def _ffn(x2d, gain, w1, w3, w2, final_gain, final_norm, side_weights=()):
    n = x2d.shape[0]
    assert n % TM_FFN == 0
    nsteps = n // TM_FFN
    row = pl.BlockSpec((TM_FFN, D_MODEL), lambda i: (i, 0))
    side_in, side_out, side_shapes = _side_cast_specs(side_weights, nsteps, lambda i: i)
    if w1.dtype == BF16:
        w_specs = [_resident(w1.shape), _resident(w3.shape), _resident(w2.shape)]
        scratch = []
    else:
        w_specs = [_HBM, _HBM, _HBM]
        scratch = [pltpu.VMEM(w1.shape, BF16), pltpu.VMEM(w3.shape, BF16), pltpu.VMEM(w2.shape, BF16)]
    return pl.pallas_call(
        functools.partial(_ffn_kernel, final_norm=final_norm, n_side=len(side_weights)),
        grid=(nsteps,),
        in_specs=[row, _resident((1, D_MODEL))] + w_specs + [_resident((1, D_MODEL))] + side_in,
        out_specs=[row] + side_out,
        out_shape=[jax.ShapeDtypeStruct((n, D_MODEL), F32)] + side_shapes,
        scratch_shapes=scratch,
        compiler_params=pltpu.CompilerParams(dimension_semantics=("arbitrary",), vmem_limit_bytes=VMEM_LIMIT),
        name="ffn_final" if final_norm else "ffn",
    )(x2d, gain, w1, w3, w2, final_gain, *side_weights)


A_WIDTH = 3 * GW_A
QSCALE_A = HD_A ** -0.5 * LOG2E
OFF_QB = N_GROUPS_A * A_WIDTH
OFF_KB = OFF_QB + N_HEADS_B * HD_B
OFF_VB = OFF_KB + N_KV_B * HD_B
OFF_GATE = OFF_VB + N_KV_B * HD_B
IN_WIDTH = OFF_GATE + 2 * D_MODEL


def _proj_kernel(x_ref, g_ref, w_ref, bg_ref, qn_ref, kn_ref, cos_ref, sin_ref,
                 a0_ref, a1_ref, a2_ref, qb_ref, kb_ref, vt_ref, gate_ref, hs_ref, hb_ref):
    tm = x_ref.shape[0]
    nslab = D_MODEL // LANES
    even_lane = (lax.broadcasted_iota(jnp.int32, (tm, HD_B), 1) & 1) == 0

    hn = _rms(x_ref[...], g_ref[...])
    hb_ref[0] = hn.astype(BF16)
    for s in range(nslab):
        hs_ref[0, s] = hn[:, s * LANES:(s + 1) * LANES]
    d_prev = 1
    for g in range(1, N_GROUPS_A):
        d = DILATED_GROUPS[g][1]
        f, lp, ln = d // d_prev, tm // d_prev, tm // d
        for r_prev in range(d_prev):
            for q in range(f):
                r = r_prev + d_prev * q
                for s in range(nslab):
                    rows = hs_ref[(g - 1) % 2, s, pl.ds(r_prev * lp + q, ln, stride=f), :]
                    hb_ref[g, r * ln:(r + 1) * ln, s * LANES:(s + 1) * LANES] = rows.astype(BF16)
                    if g + 1 < N_GROUPS_A:
                        hs_ref[g % 2, s, r * ln:(r + 1) * ln, :] = rows
        d_prev = d

    def mm(lo, g=0):
        return jnp.dot(hb_ref[g], w_ref[:, lo:lo + PROJ_CHUNK], preferred_element_type=F32)

    def _rope(y, cosf, sinf):
        partner = jnp.where(even_lane, pltpu.roll(y, HD_B - 1, axis=1), pltpu.roll(y, 1, axis=1))
        return y * cosf + partner * sinf

    def a_chunk(g, j):
        part, half = divmod(j, GW_A // PROJ_CHUNK)
        p = mm((part * N_GROUPS_A + g) * GW_A + half * PROJ_CHUNK, g)
        return p * QSCALE_A if part == 0 else p

    for j in range(2 * D_MODEL // PROJ_CHUNK):
        c = slice(j * PROJ_CHUNK, (j + 1) * PROJ_CHUNK)
        pg = mm(OFF_GATE + j * PROJ_CHUNK) + bg_ref[:, c]
        gate_ref[:, c] = (0.5 * jnp.tanh(0.5 * pg) + 0.5).astype(BF16)

    cosf = cos_ref[...]
    sinf = sin_ref[...]
    heads_per_chunk = PROJ_CHUNK // HD_B
    for j in range(N_HEADS_B // heads_per_chunk):
        pq = mm(OFF_QB + j * PROJ_CHUNK)
        for hh in range(heads_per_chunk):
            src = slice(hh * HD_B, (hh + 1) * HD_B)
            dst = slice(j * PROJ_CHUNK + hh * HD_B, j * PROJ_CHUNK + (hh + 1) * HD_B)
            qb_ref[:, dst] = _rope(_rms(pq[:, src], qn_ref[...]), cosf, sinf).astype(BF16)
    pk = mm(OFF_KB)
    for hh in range(N_KV_B):
        c = slice(hh * HD_B, (hh + 1) * HD_B)
        kb_ref[:, c] = _rope(_rms(pk[:, c], kn_ref[...]), cosf, sinf).astype(BF16)
    vt_ref[...] = mm(OFF_VB).T.astype(BF16)

    for g, a_ref in ((2, a2_ref), (1, a1_ref)):
        d = DILATED_GROUPS[g][1]
        for j in range(A_WIDTH // PROJ_CHUNK):
            c = slice(j * PROJ_CHUNK, (j + 1) * PROJ_CHUNK)
            a_ref[:, :, c] = a_chunk(g, j).reshape(d, tm // d, PROJ_CHUNK).astype(BF16)
    for j in range(A_WIDTH // PROJ_CHUNK):
        c = slice(j * PROJ_CHUNK, (j + 1) * PROJ_CHUNK)
        a0_ref[:, c] = a_chunk(0, j).astype(BF16)


def _proj(x1, gain, w_in, b_gate, qn, kn, cosf, sinf):
    bsz, seq, _ = x1.shape
    nt = seq // TM
    outs, specs = [], []
    for _, d in DILATED_GROUPS:
        outs.append(jax.ShapeDtypeStruct((bsz, d, seq // d, A_WIDTH), BF16))
        specs.append(pl.BlockSpec((None, d, TM // d, A_WIDTH), lambda b, t: (b, 0, t, 0)))
    outs += [jax.ShapeDtypeStruct((bsz, seq, N_HEADS_B * HD_B), BF16),
             jax.ShapeDtypeStruct((bsz, seq, N_KV_B * HD_B), BF16),
             jax.ShapeDtypeStruct((bsz, N_KV_B * HD_B, seq), BF16),
             jax.ShapeDtypeStruct((bsz, seq, 2 * D_MODEL), BF16)]
    specs += [pl.BlockSpec((None, TM, N_HEADS_B * HD_B), lambda b, t: (b, t, 0)),
              pl.BlockSpec((None, TM, N_KV_B * HD_B), lambda b, t: (b, t, 0)),
              pl.BlockSpec((None, N_KV_B * HD_B, TM), lambda b, t: (b, 0, t)),
              pl.BlockSpec((None, TM, 2 * D_MODEL), lambda b, t: (b, t, 0))]
    specs[0] = pl.BlockSpec((None, None, TM, A_WIDTH), lambda b, t: (b, 0, t, 0))
    return pl.pallas_call(
        _proj_kernel,
        grid=(bsz, nt),
        in_specs=[pl.BlockSpec((None, TM, D_MODEL), lambda b, t: (b, t, 0)),
                  _resident((1, D_MODEL)), _resident((D_MODEL, IN_WIDTH)), _resident((1, 2 * D_MODEL)),
                  _resident((1, HD_B)), _resident((1, HD_B)),
                  pl.BlockSpec((TM, HD_B), lambda b, t: (t, 0)),
                  pl.BlockSpec((TM, HD_B), lambda b, t: (t, 0))],
        out_specs=specs,
        out_shape=outs,
        scratch_shapes=[pltpu.VMEM((2, D_MODEL // LANES, TM, LANES), F32),
                        pltpu.VMEM((N_GROUPS_A, TM, D_MODEL), BF16)],
        compiler_params=pltpu.CompilerParams(dimension_semantics=("arbitrary", "arbitrary"),
                                             vmem_limit_bytes=VMEM_LIMIT),
        name="proj",
    )(x1, gain, w_in, b_gate, qn, kn, cosf, sinf)


def _dilated_kernel(bucket_ref, tab_ref, q_ref, k_ref, v_ref, o_ref, st_ref, bias_ref, s_ref, *, sub_len):
    nblk = sub_len // QBLK_A
    nunit = q_ref.shape[0] * nblk
    npair = HEADS_A // 2

    @pl.when((pl.program_id(0) == 0) & (pl.program_id(1) == 0))
    def _build_bias():
        for v in range(3):
            bt = bucket_ref[v]
            accs = [jnp.full((QBLK_A, KWIN_A), NEG_INF, F32)] * HEADS_A
            for j in range(N_BUCKETS):
                hit = bt == j
                accs = [jnp.where(hit, tab_ref[j, hh] * LOG2E, acc) for hh, acc in enumerate(accs)]
            for hh, acc in enumerate(accs):
                bias_ref[v, hh // 2, (hh % 2) * QBLK_A:(hh % 2 + 1) * QBLK_A, :] = acc

    lane = lax.broadcasted_iota(jnp.int32, (QBLK_A, LANES), 1)
    low = lane < HD_A

    def geometry(u):
        r = u >> (nblk.bit_length() - 1)
        n = u & (nblk - 1)
        q0 = pl.multiple_of(n * QBLK_A, QBLK_A)
        w0 = pl.multiple_of(jnp.clip(n * QBLK_A - (KWIN_A - QBLK_A) // 2, 0, sub_len - KWIN_A), 64)
        variant = jnp.where(n == 0, 0, jnp.where(n == nblk - 1, 2, 1))
        return r, q0, w0, variant

    def score_pass(u, pair):
        r, q0, w0, variant = geometry(u)
        c = slice(pair * LANES, (pair + 1) * LANES)
        qp = q_ref[r, pl.ds(q0, QBLK_A), c]
        kp = k_ref[r, pl.ds(w0, KWIN_A), c]
        zero = jnp.zeros_like(qp)
        q2 = jnp.concatenate([jnp.where(low, qp, zero), jnp.where(low, zero, qp)], axis=0)
        s = lax.dot_general(q2, kp, (((1,), (1,)), ((), ())), preferred_element_type=F32)
        s = s + bias_ref[variant, pair]
        s_ref[pair] = s
        return jnp.max(s, axis=-1, keepdims=True)

    def value_pass(u, pair, m, stats):
        r, q0, w0, _ = geometry(u)
        c = slice(pair * LANES, (pair + 1) * LANES)
        vp = v_ref[r, pl.ds(w0, KWIN_A), c]
        p = jnp.exp2(s_ref[pair] - m)
        l = jnp.sum(p, axis=-1, keepdims=True)
        o = jnp.dot(p.astype(BF16), vp, preferred_element_type=F32)
        o_ref[r, pl.ds(q0, QBLK_A), c] = jnp.where(low, o[:QBLK_A], o[QBLK_A:]).astype(BF16)
        for half in range(2):
            rows = slice(half * QBLK_A, (half + 1) * QBLK_A)
            hh = 2 * pair + half
            stats = jnp.where(lane == hh, m[rows], stats)
            stats = jnp.where(lane == HEADS_A + hh, l[rows], stats)
        return stats

    def body(u, ms):
        stats = jnp.zeros((QBLK_A, LANES), F32)
        for pair in range(npair):
            if pair + 1 < npair:
                nxt = score_pass(u, pair + 1)
            else:
                nxt = score_pass(jnp.minimum(u + 1, nunit - 1), 0)
            stats = value_pass(u, pair, ms, stats)
            ms = nxt
        r, q0, _, _ = geometry(u)
        st_ref[r, pl.ds(q0, QBLK_A), :] = stats
        return ms

    lax.fori_loop(0, nunit, body, score_pass(0, 0), unroll=DILATED_UNROLL)


def _dilated(a_g, buckets, table, g):
    bsz, d, sub_len, _ = a_g.shape
    nblk = sub_len // QBLK_A
    assert sub_len % QBLK_A == 0 and sub_len >= KWIN_A and nblk & (nblk - 1) == 0
    rb = min(d, max(1, DILATED_ROWS // sub_len))
    assert d % rb == 0
    blk = lambda col: pl.BlockSpec((None, rb, sub_len, GW_A), lambda b, r: (b, r, 0, col))
    return pl.pallas_call(
        functools.partial(_dilated_kernel, sub_len=sub_len),
        grid=(bsz, d // rb),
        in_specs=[_resident((3, QBLK_A, KWIN_A)),
                  pl.BlockSpec(memory_space=pltpu.SMEM),
                  blk(0), blk(1), blk(2)],
        out_specs=[blk(0), pl.BlockSpec((None, rb, sub_len, LANES), lambda b, r: (b, r, 0, 0))],
        out_shape=[jax.ShapeDtypeStruct((bsz, d, sub_len, GW_A), BF16),
                   jax.ShapeDtypeStruct((bsz, d, sub_len, LANES), F32)],
        scratch_shapes=[pltpu.VMEM((3, HEADS_A // 2, 2 * QBLK_A, KWIN_A), F32),
                        pltpu.VMEM((HEADS_A // 2, 2 * QBLK_A, KWIN_A), F32)],
        compiler_params=pltpu.CompilerParams(dimension_semantics=("arbitrary", "arbitrary"),
                                             vmem_limit_bytes=VMEM_LIMIT),
        name=f"dilated{g}",
    )(buckets, table, a_g, a_g, a_g)


def _gqa_kernel(q_ref, k_ref, vt_ref, *rest, n_side):
    side_in, (o_ref, *side_out), (s_ref, p_ref) = rest[:n_side], rest[n_side:2 * n_side + 1], rest[2 * n_side + 1:]
    _side_casts(side_in, side_out)
    seq = k_ref.shape[0]
    nch = seq // KCH_B
    nqt = seq // TQ_B

    def score_pass(qt, g):
        q0 = pl.multiple_of(qt * TQ_B, TQ_B)
        q = q_ref[pl.ds(q0, TQ_B), g * HD_B:(g + 1) * HD_B]
        mrun = jnp.full((8, TQ_B), NEG_INF, F32)
        for ch in range(nch):
            rows = slice(ch * KCH_B, (ch + 1) * KCH_B)
            st = lax.dot_general(k_ref[rows, :], q, (((1,), (1,)), ((), ())),
                                 preferred_element_type=F32)
            s_ref[g, rows, :] = st
            mrun = jnp.maximum(mrun, jnp.max(st.reshape(KCH_B // 8, 8, TQ_B), axis=0))
        return mrun

    def value_pass(qt, g, mrun):
        q0 = pl.multiple_of(qt * TQ_B, TQ_B)
        m = jnp.max(mrun, axis=0, keepdims=True)
        lrun = jnp.zeros((8, TQ_B), F32)
        for ch in range(nch):
            rows = slice(ch * KCH_B, (ch + 1) * KCH_B)
            p = jnp.exp2(s_ref[g, rows, :] - m)
            lrun = lrun + jnp.sum(p.reshape(KCH_B // 8, 8, TQ_B), axis=0)
            p_ref[g, rows, :] = p.astype(BF16)
        l = jnp.sum(lrun, axis=0, keepdims=True)
        ot = jnp.dot(vt_ref[...], p_ref[g], preferred_element_type=F32)
        o_ref[pl.ds(q0, TQ_B), g * HD_B:(g + 1) * HD_B] = (ot / l).T.astype(BF16)

    def body(qt, mrun):
        for g in range(GQA_B):
            if g + 1 < GQA_B:
                nxt = score_pass(qt, g + 1)
            else:
                nxt = score_pass(jnp.minimum(qt + 1, nqt - 1), 0)
            value_pass(qt, g, mrun)
            mrun = nxt
        return mrun

    lax.fori_loop(0, nqt, body, score_pass(0, 0), unroll=GQA_UNROLL)


def _gqa(qb, kb, vt, side_weights=()):
    bsz, seq, _ = qb.shape
    gw = GQA_B * HD_B
    side_in, side_out, side_shapes = _side_cast_specs(side_weights, bsz * N_KV_B, lambda b, kv: b * N_KV_B + kv)
    return pl.pallas_call(
        functools.partial(_gqa_kernel, n_side=len(side_weights)),
        grid=(bsz, N_KV_B),
        in_specs=[pl.BlockSpec((None, seq, gw), lambda b, kv: (b, 0, kv)),
                  pl.BlockSpec((None, seq, HD_B), lambda b, kv: (b, 0, kv)),
                  pl.BlockSpec((None, HD_B, seq), lambda b, kv: (b, kv, 0))] + side_in,
        out_specs=[pl.BlockSpec((None, seq, gw), lambda b, kv: (b, 0, kv))] + side_out,
        out_shape=[jax.ShapeDtypeStruct((bsz, seq, N_HEADS_B * HD_B), BF16)] + side_shapes,
        scratch_shapes=[pltpu.VMEM((GQA_B, seq, TQ_B), F32), pltpu.VMEM((GQA_B, seq, TQ_B), BF16)],
        compiler_params=pltpu.CompilerParams(dimension_semantics=("arbitrary", "arbitrary"),
                                             vmem_limit_bytes=VMEM_LIMIT),
        name="gqa",
    )(qb, kb, vt, *side_weights)


def _merge_kernel(x_ref, gate_ref, ob_ref, o0_ref, s0_ref, o1_ref, s1_ref, o2_ref, s2_ref,
                  wa_ref, wb_ref, wo_ref, expand_ref, out_ref, oslab_ref, sslab_ref, tmp_ref):
    tm = x_ref.shape[0]
    nslab = GW_A // LANES

    def planes(g, src_o, src_s, dst_o, dst_s, lp):
        d = DILATED_GROUPS[g][1]
        d_dst = d // 4
        for r_dst in range(d_dst):
            for q in range(4):
                rows = pl.ds(r_dst * 4 * lp + q, lp, stride=4)
                dst_s(rows, src_s(r_dst + d_dst * q))
                for s in range(nslab):
                    dst_o(s, rows, src_o(r_dst + d_dst * q, s))

    lane_blk = lambda s: slice(s * LANES, (s + 1) * LANES)

    def put_tmp_o(s, rows, val):
        tmp_ref[s, rows, :] = val

    def put_tmp_s(rows, val):
        tmp_ref[nslab, rows, :] = val

    def put_slab_o(g):
        def put(s, rows, val):
            oslab_ref[g, s, rows, :] = val
        return put

    def put_slab_s(g):
        def put(rows, val):
            sslab_ref[g, rows, :] = val
        return put

    l1, l2 = tm // DILATED_GROUPS[1][1], tm // DILATED_GROUPS[2][1]
    planes(1, lambda r, s: o1_ref[r, :, lane_blk(s)].astype(F32), lambda r: s1_ref[r],
           put_slab_o(0), put_slab_s(0), l1)
    planes(2, lambda r, s: o2_ref[r, :, lane_blk(s)].astype(F32), lambda r: s2_ref[r],
           put_tmp_o, put_tmp_s, l2)
    planes(1, lambda r, s: tmp_ref[s, r * l1:(r + 1) * l1, :], lambda r: tmp_ref[nslab, r * l1:(r + 1) * l1, :],
           put_slab_o(1), put_slab_s(1), l1)

    stats = [s0_ref[...], sslab_ref[0], sslab_ref[1]]
    dens = [pltpu.roll(st, LANES - HEADS_A, axis=1) for st in stats]
    mx = jnp.maximum(jnp.maximum(stats[0], stats[1]), stats[2])
    es = [jnp.exp2(st - mx) for st in stats]
    total = es[0] * dens[0] + es[1] * dens[1] + es[2] * dens[2]
    head_lane = lax.broadcasted_iota(jnp.int32, (tm, LANES), 1) < HEADS_A
    o_a = jnp.zeros((tm, GW_A), F32)
    for g in range(N_GROUPS_A):
        w = jnp.where(head_lane, es[g] / total, 0.0)
        w_hi = w.astype(BF16)
        w_lo = (w - w_hi.astype(F32)).astype(BF16)
        wide = jnp.dot(jnp.concatenate([w_hi, w_lo], axis=1), expand_ref[...], preferred_element_type=F32)
        if g == 0:
            acc = o0_ref[...].astype(F32)
        else:
            acc = jnp.concatenate([oslab_ref[g - 1, s] for s in range(nslab)], axis=1)
        o_a = o_a + wide * acc
    o_a = o_a.astype(BF16)

    ya = jnp.dot(o_a, wa_ref[...], preferred_element_type=F32)
    yb = jnp.dot(ob_ref[...], wb_ref[...], preferred_element_type=F32)
    merged = (gate_ref[:, :D_MODEL].astype(F32) * ya + gate_ref[:, D_MODEL:].astype(F32) * yb).astype(BF16)
    out_ref[...] = x_ref[...] + jnp.dot(merged, wo_ref[...], preferred_element_type=F32)


def _merge(x1, gates, ob, outs, stats, wa, wb, wo):
    bsz, seq, _ = x1.shape
    tok = lambda w: pl.BlockSpec((None, TM, w), lambda b, t: (b, t, 0))
    a_specs = []
    for _, d in DILATED_GROUPS:
        for width in (GW_A, LANES):
            if d == 1:
                a_specs.append(pl.BlockSpec((None, None, TM, width), lambda b, t: (b, 0, t, 0)))
            else:
                a_specs.append(pl.BlockSpec((None, d, TM // d, width), lambda b, t: (b, 0, t, 0)))
    a_args = [arr for pair in zip(outs, stats) for arr in pair]
    src = jnp.arange(2 * LANES, dtype=jnp.int32)[:, None] % LANES
    dst = jnp.arange(GW_A, dtype=jnp.int32)[None, :] // HD_A
    expand = (src == dst).astype(BF16)
    return pl.pallas_call(
        _merge_kernel,
        grid=(bsz, seq // TM),
        in_specs=[tok(D_MODEL), tok(2 * D_MODEL), tok(N_HEADS_B * HD_B)] + a_specs
                 + [_resident((GW_A, D_MODEL)), _resident((N_HEADS_B * HD_B, D_MODEL)),
                    _resident((D_MODEL, D_MODEL)), _resident((2 * LANES, GW_A))],
        out_specs=tok(D_MODEL),
        out_shape=jax.ShapeDtypeStruct((bsz, seq, D_MODEL), F32),
        scratch_shapes=[pltpu.VMEM((2, GW_A // LANES, TM, LANES), F32),
                        pltpu.VMEM((2, TM, LANES), F32),
                        pltpu.VMEM((GW_A // LANES + 1, TM, LANES), F32)],
        compiler_params=pltpu.CompilerParams(dimension_semantics=("arbitrary", "arbitrary"),
                                             vmem_limit_bytes=VMEM_LIMIT),
        name="merge",
    )(x1, gates, ob, *a_args, wa, wb, wo, expand)


def _t5_bucket(rel):
    n = N_BUCKETS // 2
    max_exact = n // 2
    ret = jnp.where(rel > 0, n, 0)
    a = jnp.abs(rel)
    af = jnp.maximum(a, 1).astype(F32)
    large = max_exact + (jnp.log(af / max_exact) / math.log(MAX_DISTANCE / max_exact)
                         * (n - max_exact)).astype(jnp.int32)
    large = jnp.minimum(large, n - 1)
    return ret + jnp.where(a < max_exact, a, large)


def _bucket_tiles(dilation, half):
    qi = jnp.arange(QBLK_A, dtype=jnp.int32)[:, None]
    kj = jnp.arange(KWIN_A, dtype=jnp.int32)[None, :]
    tiles = []
    for shift in (0, (KWIN_A - QBLK_A) // 2, KWIN_A - QBLK_A):
        rel = kj - shift - qi
        tiles.append(jnp.where(jnp.abs(rel) <= half, _t5_bucket(rel * dilation), -1))
    return jnp.stack(tiles)


def _rope_tables(seq):
    t = np.arange(seq)
    n_freq = HD_B // 4
    freq = (ROPE_THETA ** (-np.arange(n_freq, dtype=np.float32) / n_freq)).astype(np.float32)
    ang = np.concatenate([(t // GRID_W).astype(np.float32)[:, None] * freq,
                          (t % GRID_W).astype(np.float32)[:, None] * freq], -1)
    cos, sin = np.cos(ang), np.sin(ang)
    cosf = np.repeat(cos, 2, axis=-1)
    sinf = np.stack([-sin, sin], axis=-1).reshape(seq, HD_B)
    return jnp.asarray(cosf, F32), jnp.asarray(sinf, F32)


def kernel(x, ffn1_norm, ffn1_w1, ffn1_w3, ffn1_w2, mix_norm, w_in, b_gate, q_norm, k_norm, rel_bias,
           w_branch_a, w_branch_b, w_out, ffn2_norm, ffn2_w1, ffn2_w3, ffn2_w2, final_norm):
    bsz, seq, dm = x.shape
    assert dm == D_MODEL and seq % TM == 0 and ffn1_w1.shape[0] == 1
    row = lambda v: v.reshape(1, -1).astype(F32)
    cosf, sinf = _rope_tables(seq)
    qn = row(q_norm[0] * (HD_B ** -0.5 * LOG2E))
    kn = row(k_norm[0])

    x1, w_in_bf = _ffn(x.reshape(bsz * seq, dm), row(ffn1_norm[0]), ffn1_w1[0], ffn1_w3[0], ffn1_w2[0],
                       row(final_norm), False, side_weights=(w_in[0],))
    x1 = x1.reshape(bsz, seq, dm)
    a0, a1, a2, qb, kb, vt, gates = _proj(x1, row(mix_norm[0]), w_in_bf, row(b_gate[0]), qn, kn, cosf, sinf)

    outs, stats = [], []
    for g, (a_g, (window, d)) in enumerate(zip((a0, a1, a2), DILATED_GROUPS)):
        table = rel_bias[:, g * HEADS_A:(g + 1) * HEADS_A].astype(F32)
        o_g, st_g = _dilated(a_g, _bucket_tiles(d, window // (2 * d)), table, g)
        outs.append(o_g)
        stats.append(st_g)
    ob, wa, wb, wo, w1, w3, w2 = _gqa(qb, kb, vt, side_weights=(
        w_branch_a[0], w_branch_b[0], w_out[0], ffn2_w1[0], ffn2_w3[0], ffn2_w2[0]))

    x2 = _merge(x1, gates, ob, outs, stats, wa, wb, wo)
    y, = _ffn(x2.reshape(bsz * seq, dm), row(ffn2_norm[0]), w1, w3, w2, row(final_norm), True)
    return y.reshape(bsz, seq, dm)
```

```python
import functools
import math

import jax
import jax.numpy as jnp
import numpy as np
from jax import lax
from jax.experimental import pallas as pl
from jax.experimental.pallas import tpu as pltpu

D_MODEL = 1024
D_FF = 2816
DILATED_GROUPS = ((128, 1), (512, 4), (2048, 16))
N_GROUPS_A = 3
HEADS_A = 8
HD_A = 64
GW_A = HEADS_A * HD_A
N_HEADS_B = 8
N_KV_B = 2
GQA_B = N_HEADS_B // N_KV_B
HD_B = 128
GRID_W = 64
ROPE_THETA = 10000.0
N_BUCKETS = 32
MAX_DISTANCE = 1024
EPS = 1e-6
NEG_INF = -1e30
LOG2E = 1.4426950408889634

LANES = 128
TM = 512
TM_FFN = 1024
FFN_CHUNK = 256
PROJ_CHUNK = 256
QBLK_A = 128
KWIN_A = 256
DILATED_ROWS = 4096
DILATED_UNROLL = 16
TQ_B = 256
KCH_B = 512
GQA_UNROLL = 4
VMEM_LIMIT = 60 * 1024 * 1024
WEIGHT_STAGE_BYTES = 2 * 1024 * 1024
WEIGHT_STAGE_SLOTS = 4

BF16 = jnp.bfloat16
F32 = jnp.float32


_HBM = pl.BlockSpec(memory_space=pl.ANY)


def _resident(shape):
    nd = len(shape)
    return pl.BlockSpec(shape, lambda *_: (0,) * nd, pipeline_mode=pl.Buffered(1))


def _rms(x, gain):
    ms = jnp.mean(x * x, axis=-1, keepdims=True)
    return x * lax.rsqrt(ms + EPS) * gain


def _load_weight_bf16(src_hbm, dst_ref):
    rows, cols = src_hbm.shape
    chunk_rows = max(r for r in range(16, rows + 1, 16)
                     if rows % r == 0 and r * cols * 4 <= WEIGHT_STAGE_BYTES)
    nchunk = rows // chunk_rows

    ahead = WEIGHT_STAGE_SLOTS - 1

    def run(stage_ref, sem_ref):
        def copy(i):
            slot = i % WEIGHT_STAGE_SLOTS
            return pltpu.make_async_copy(src_hbm.at[pl.ds(i * chunk_rows, chunk_rows), :],
                                         stage_ref.at[slot], sem_ref.at[slot])
        for i in range(min(ahead, nchunk)):
            copy(i).start()
        for i in range(nchunk):
            if i + ahead < nchunk:
                copy(i + ahead).start()
            copy(i).wait()
            dst_ref[i * chunk_rows:(i + 1) * chunk_rows, :] = stage_ref[i % WEIGHT_STAGE_SLOTS].astype(BF16)

    pl.run_scoped(run, pltpu.VMEM((WEIGHT_STAGE_SLOTS, chunk_rows, cols), F32),
                  pltpu.SemaphoreType.DMA((WEIGHT_STAGE_SLOTS,)))


def _side_cast_specs(weights, nsteps, step_of):
    in_specs, out_specs, out_shapes = [], [], []
    for w in weights:
        rows, cols = w.shape
        assert rows % (16 * nsteps) == 0, (w.shape, nsteps)
        index = lambda *ids: (step_of(*ids), 0)
        in_specs.append(pl.BlockSpec((rows // nsteps, cols), index))
        out_specs.append(pl.BlockSpec((rows // nsteps, cols), index))
        out_shapes.append(jax.ShapeDtypeStruct((rows, cols), BF16))
    return in_specs, out_specs, out_shapes


def _side_casts(side_in, side_out):
    for src, dst in zip(side_in, side_out, strict=True):
        dst[...] = src[...].astype(BF16)


def _ffn_kernel(x_ref, g_ref, w1_in, w3_in, w2_in, fg_ref, *rest, final_norm, n_side):
    side_in, (o_ref, *side_out), scratch = rest[:n_side], rest[n_side:2 * n_side + 1], rest[2 * n_side + 1:]
    _side_casts(side_in, side_out)
    if scratch:
        w1_ref, w3_ref, w2_ref = scratch

        @pl.when(pl.program_id(0) == 0)
        def _load_weights():
            _load_weight_bf16(w1_in, w1_ref)
            _load_weight_bf16(w3_in, w3_ref)
            _load_weight_bf16(w2_in, w2_ref)
    else:
        w1_ref, w3_ref, w2_ref = w1_in, w3_in, w2_in

    x = x_ref[...]
    h = _rms(x, g_ref[...]).astype(BF16)
    acc = jnp.zeros((x.shape[0], D_MODEL), F32)
    for j in range(D_FF // FFN_CHUNK):
        c = slice(j * FFN_CHUNK, (j + 1) * FFN_CHUNK)
        a = jnp.dot(h, w1_ref[:, c], preferred_element_type=F32)
        b = jnp.dot(h, w3_ref[:, c], preferred_element_type=F32)
        g = (a * jax.nn.sigmoid(a) * b).astype(BF16)
        acc = acc + jnp.dot(g, w2_ref[c, :], preferred_element_type=F32)
    y = x + 0.5 * acc
    if final_norm:
        y = _rms(y, fg_ref[...])
    o_ref[...] = y


def _ffn(x2d, gain, w1, w3, w2, final_gain, final_norm, side_weights=()):
    n = x2d.shape[0]
    assert n % TM_FFN == 0
    nsteps = n // TM_FFN
    row = pl.BlockSpec((TM_FFN, D_MODEL), lambda i: (i, 0))
    side_in, side_out, side_shapes = _side_cast_specs(side_weights, nsteps, lambda i: i)
    if w1.dtype == BF16:
        w_specs = [_resident(w1.shape), _resident(w3.shape), _resident(w2.shape)]
        scratch = []
    else:
        w_specs = [_HBM, _HBM, _HBM]
        scratch = [pltpu.VMEM(w1.shape, BF16), pltpu.VMEM(w3.shape, BF16), pltpu.VMEM(w2.shape, BF16)]
    return pl.pallas_call(
        functools.partial(_ffn_kernel, final_norm=final_norm, n_side=len(side_weights)),
        grid=(nsteps,),
        in_specs=[row, _resident((1, D_MODEL))] + w_specs + [_resident((1, D_MODEL))] + side_in,
        out_specs=[row] + side_out,
        out_shape=[jax.ShapeDtypeStruct((n, D_MODEL), F32)] + side_shapes,
        scratch_shapes=scratch,
        compiler_params=pltpu.CompilerParams(dimension_semantics=("arbitrary",), vmem_limit_bytes=VMEM_LIMIT),
        name="ffn_final" if final_norm else "ffn",
    )(x2d, gain, w1, w3, w2, final_gain, *side_weights)


A_WIDTH = 3 * GW_A
QSCALE_A = HD_A ** -0.5 * LOG2E
OFF_QB = N_GROUPS_A * A_WIDTH
OFF_KB = OFF_QB + N_HEADS_B * HD_B
OFF_VB = OFF_KB + N_KV_B * HD_B
OFF_GATE = OFF_VB + N_KV_B * HD_B
IN_WIDTH = OFF_GATE + 2 * D_MODEL


def _proj_kernel(x_ref, g_ref, w_ref, bg_ref, qn_ref, kn_ref, cos_ref, sin_ref,
                 a0_ref, a1_ref, a2_ref, qb_ref, kb_ref, vt_ref, gate_ref, hs_ref, hb_ref):
    tm = x_ref.shape[0]
    nslab = D_MODEL // LANES
    even_lane = (lax.broadcasted_iota(jnp.int32, (tm, HD_B), 1) & 1) == 0

    hn = _rms(x_ref[...], g_ref[...])
    hb_ref[0] = hn.astype(BF16)
    for s in range(nslab):
        hs_ref[0, s] = hn[:, s * LANES:(s + 1) * LANES]
    d_prev = 1
    for g in range(1, N_GROUPS_A):
        d = DILATED_GROUPS[g][1]
        f, lp, ln = d // d_prev, tm // d_prev, tm // d
        for r_prev in range(d_prev):
            for q in range(f):
                r = r_prev + d_prev * q
                for s in range(nslab):
                    rows = hs_ref[(g - 1) % 2, s, pl.ds(r_prev * lp + q, ln, stride=f), :]
                    hb_ref[g, r * ln:(r + 1) * ln, s * LANES:(s + 1) * LANES] = rows.astype(BF16)
                    if g + 1 < N_GROUPS_A:
                        hs_ref[g % 2, s, r * ln:(r + 1) * ln, :] = rows
        d_prev = d

    def mm(lo, g=0):
        return jnp.dot(hb_ref[g], w_ref[:, lo:lo + PROJ_CHUNK], preferred_element_type=F32)

    def _rope(y, cosf, sinf):
        partner = jnp.where(even_lane, pltpu.roll(y, HD_B - 1, axis=1), pltpu.roll(y, 1, axis=1))
        return y * cosf + partner * sinf

    def a_chunk(g, j):
        part, half = divmod(j, GW_A // PROJ_CHUNK)
        p = mm((part * N_GROUPS_A + g) * GW_A + half * PROJ_CHUNK, g)
        return p * QSCALE_A if part == 0 else p

    for j in range(2 * D_MODEL // PROJ_CHUNK):
        c = slice(j * PROJ_CHUNK, (j + 1) * PROJ_CHUNK)
        pg = mm(OFF_GATE + j * PROJ_CHUNK) + bg_ref[:, c]
        gate_ref[:, c] = (0.5 * jnp.tanh(0.5 * pg) + 0.5).astype(BF16)

    cosf = cos_ref[...]
    sinf = sin_ref[...]
    heads_per_chunk = PROJ_CHUNK // HD_B
    for j in range(N_HEADS_B // heads_per_chunk):
        pq = mm(OFF_QB + j * PROJ_CHUNK)
        for hh in range(heads_per_chunk):
            src = slice(hh * HD_B, (hh + 1) * HD_B)
            dst = slice(j * PROJ_CHUNK + hh * HD_B, j * PROJ_CHUNK + (hh + 1) * HD_B)
            qb_ref[:, dst] = _rope(_rms(pq[:, src], qn_ref[...]), cosf, sinf).astype(BF16)
    pk = mm(OFF_KB)
    for hh in range(N_KV_B):
        c = slice(hh * HD_B, (hh + 1) * HD_B)
        kb_ref[:, c] = _rope(_rms(pk[:, c], kn_ref[...]), cosf, sinf).astype(BF16)
    vt_ref[...] = mm(OFF_VB).T.astype(BF16)

    for g, a_ref in ((2, a2_ref), (1, a1_ref)):
        d = DILATED_GROUPS[g][1]
        for j in range(A_WIDTH // PROJ_CHUNK):
            c = slice(j * PROJ_CHUNK, (j + 1) * PROJ_CHUNK)
            a_ref[:, :, c] = a_chunk(g, j).reshape(d, tm // d, PROJ_CHUNK).astype(BF16)
    for j in range(A_WIDTH // PROJ_CHUNK):
        c = slice(j * PROJ_CHUNK, (j + 1) * PROJ_CHUNK)
        a0_ref[:, c] = a_chunk(0, j).astype(BF16)


def _proj(x1, gain, w_in, b_gate, qn, kn, cosf, sinf):
    bsz, seq, _ = x1.shape
    nt = seq // TM
    outs, specs = [], []
    for _, d in DILATED_GROUPS:
        outs.append(jax.ShapeDtypeStruct((bsz, d, seq // d, A_WIDTH), BF16))
        specs.append(pl.BlockSpec((None, d, TM // d, A_WIDTH), lambda b, t: (b, 0, t, 0)))
    outs += [jax.ShapeDtypeStruct((bsz, seq, N_HEADS_B * HD_B), BF16),
             jax.ShapeDtypeStruct((bsz, seq, N_KV_B * HD_B), BF16),
             jax.ShapeDtypeStruct((bsz, N_KV_B * HD_B, seq), BF16),
             jax.ShapeDtypeStruct((bsz, seq, 2 * D_MODEL), BF16)]
    specs += [pl.BlockSpec((None, TM, N_HEADS_B * HD_B), lambda b, t: (b, t, 0)),
              pl.BlockSpec((None, TM, N_KV_B * HD_B), lambda b, t: (b, t, 0)),
              pl.BlockSpec((None, N_KV_B * HD_B, TM), lambda b, t: (b, 0, t)),
              pl.BlockSpec((None, TM, 2 * D_MODEL), lambda b, t: (b, t, 0))]
    specs[0] = pl.BlockSpec((None, None, TM, A_WIDTH), lambda b, t: (b, 0, t, 0))
    return pl.pallas_call(
        _proj_kernel,
        grid=(bsz, nt),
        in_specs=[pl.BlockSpec((None, TM, D_MODEL), lambda b, t: (b, t, 0)),
                  _resident((1, D_MODEL)), _resident((D_MODEL, IN_WIDTH)), _resident((1, 2 * D_MODEL)),
                  _resident((1, HD_B)), _resident((1, HD_B)),
                  pl.BlockSpec((TM, HD_B), lambda b, t: (t, 0)),
                  pl.BlockSpec((TM, HD_B), lambda b, t: (t, 0))],
        out_specs=specs,
        out_shape=outs,
        scratch_shapes=[pltpu.VMEM((2, D_MODEL // LANES, TM, LANES), F32),
                        pltpu.VMEM((N_GROUPS_A, TM, D_MODEL), BF16)],
        compiler_params=pltpu.CompilerParams(dimension_semantics=("arbitrary", "arbitrary"),
                                             vmem_limit_bytes=VMEM_LIMIT),
        name="proj",
    )(x1, gain, w_in, b_gate, qn, kn, cosf, sinf)


def _dilated_kernel(bucket_ref, tab_ref, q_ref, k_ref, v_ref, o_ref, st_ref, bias_ref, s_ref, *, sub_len):
    nblk = sub_len // QBLK_A
    nunit = q_ref.shape[0] * nblk
    npair = HEADS_A // 2

    @pl.when((pl.program_id(0) == 0) & (pl.program_id(1) == 0))
    def _build_bias():
        for v in range(3):
            bt = bucket_ref[v]
            accs = [jnp.full((QBLK_A, KWIN_A), NEG_INF, F32)] * HEADS_A
            for j in range(N_BUCKETS):
                hit = bt == j
                accs = [jnp.where(hit, tab_ref[j, hh] * LOG2E, acc) for hh, acc in enumerate(accs)]
            for hh, acc in enumerate(accs):
                bias_ref[v, hh // 2, (hh % 2) * QBLK_A:(hh % 2 + 1) * QBLK_A, :] = acc

    lane = lax.broadcasted_iota(jnp.int32, (QBLK_A, LANES), 1)
    low = lane < HD_A

    def geometry(u):
        r = u >> (nblk.bit_length() - 1)
        n = u & (nblk - 1)
        q0 = pl.multiple_of(n * QBLK_A, QBLK_A)
        w0 = pl.multiple_of(jnp.clip(n * QBLK_A - (KWIN_A - QBLK_A) // 2, 0, sub_len - KWIN_A), 64)
        variant = jnp.where(n == 0, 0, jnp.where(n == nblk - 1, 2, 1))
        return r, q0, w0, variant

    def score_pass(u, pair):
        r, q0, w0, variant = geometry(u)
        c = slice(pair * LANES, (pair + 1) * LANES)
        qp = q_ref[r, pl.ds(q0, QBLK_A), c]
        kp = k_ref[r, pl.ds(w0, KWIN_A), c]
        zero = jnp.zeros_like(qp)
        q2 = jnp.concatenate([jnp.where(low, qp, zero), jnp.where(low, zero, qp)], axis=0)
        s = lax.dot_general(q2, kp, (((1,), (1,)), ((), ())), preferred_element_type=F32)
        s = s + bias_ref[variant, pair]
        s_ref[pair] = s
        return jnp.max(s, axis=-1, keepdims=True)

    def value_pass(u, pair, m, stats):
        r, q0, w0, _ = geometry(u)
        c = slice(pair * LANES, (pair + 1) * LANES)
        vp = v_ref[r, pl.ds(w0, KWIN_A), c]
        p = jnp.exp2(s_ref[pair] - m)
        l = jnp.sum(p, axis=-1, keepdims=True)
        o = jnp.dot(p.astype(BF16), vp, preferred_element_type=F32)
        o_ref[r, pl.ds(q0, QBLK_A), c] = jnp.where(low, o[:QBLK_A], o[QBLK_A:]).astype(BF16)
        for half in range(2):
            rows = slice(half * QBLK_A, (half + 1) * QBLK_A)
            hh = 2 * pair + half
            stats = jnp.where(lane == hh, m[rows], stats)
            stats = jnp.where(lane == HEADS_A + hh, l[rows], stats)
        return stats

    def body(u, ms):
        stats = jnp.zeros((QBLK_A, LANES), F32)
        for pair in range(npair):
            if pair + 1 < npair:
                nxt = score_pass(u, pair + 1)
            else:
                nxt = score_pass(jnp.minimum(u + 1, nunit - 1), 0)
            stats = value_pass(u, pair, ms, stats)
            ms = nxt
        r, q0, _, _ = geometry(u)
        st_ref[r, pl.ds(q0, QBLK_A), :] = stats
        return ms

    lax.fori_loop(0, nunit, body, score_pass(0, 0), unroll=DILATED_UNROLL)


def _dilated(a_g, buckets, table, g):
    bsz, d, sub_len, _ = a_g.shape
    nblk = sub_len // QBLK_A
    assert sub_len % QBLK_A == 0 and sub_len >= KWIN_A and nblk & (nblk - 1) == 0
    rb = min(d, max(1, DILATED_ROWS // sub_len))
    assert d % rb == 0
    blk = lambda col: pl.BlockSpec((None, rb, sub_len, GW_A), lambda b, r: (b, r, 0, col))
    return pl.pallas_call(
        functools.partial(_dilated_kernel, sub_len=sub_len),
        grid=(bsz, d // rb),
        in_specs=[_resident((3, QBLK_A, KWIN_A)),
                  pl.BlockSpec(memory_space=pltpu.SMEM),
                  blk(0), blk(1), blk(2)],
        out_specs=[blk(0), pl.BlockSpec((None, rb, sub_len, LANES), lambda b, r: (b, r, 0, 0))],
        out_shape=[jax.ShapeDtypeStruct((bsz, d, sub_len, GW_A), BF16),
                   jax.ShapeDtypeStruct((bsz, d, sub_len, LANES), F32)],
        scratch_shapes=[pltpu.VMEM((3, HEADS_A // 2, 2 * QBLK_A, KWIN_A), F32),
                        pltpu.VMEM((HEADS_A // 2, 2 * QBLK_A, KWIN_A), F32)],
        compiler_params=pltpu.CompilerParams(dimension_semantics=("arbitrary", "arbitrary"),
                                             vmem_limit_bytes=VMEM_LIMIT),
        name=f"dilated{g}",
    )(buckets, table, a_g, a_g, a_g)


def _gqa_kernel(q_ref, k_ref, vt_ref, *rest, n_side):
    side_in, (o_ref, *side_out), (s_ref, p_ref) = rest[:n_side], rest[n_side:2 * n_side + 1], rest[2 * n_side + 1:]
    _side_casts(side_in, side_out)
    seq = k_ref.shape[0]
    nch = seq // KCH_B
    nqt = seq // TQ_B

    def score_pass(qt, g):
        q0 = pl.multiple_of(qt * TQ_B, TQ_B)
        q = q_ref[pl.ds(q0, TQ_B), g * HD_B:(g + 1) * HD_B]
        mrun = jnp.full((8, TQ_B), NEG_INF, F32)
        for ch in range(nch):
            rows = slice(ch * KCH_B, (ch + 1) * KCH_B)
            st = lax.dot_general(k_ref[rows, :], q, (((1,), (1,)), ((), ())),
                                 preferred_element_type=F32)
            s_ref[g, rows, :] = st
            mrun = jnp.maximum(mrun, jnp.max(st.reshape(KCH_B // 8, 8, TQ_B), axis=0))
        return mrun

    def value_pass(qt, g, mrun):
        q0 = pl.multiple_of(qt * TQ_B, TQ_B)
        m = jnp.max(mrun, axis=0, keepdims=True)
        lrun = jnp.zeros((8, TQ_B), F32)
        for ch in range(nch):
            rows = slice(ch * KCH_B, (ch + 1) * KCH_B)
            p = jnp.exp2(s_ref[g, rows, :] - m)
            lrun = lrun + jnp.sum(p.reshape(KCH_B // 8, 8, TQ_B), axis=0)
            p_ref[g, rows, :] = p.astype(BF16)
        l = jnp.sum(lrun, axis=0, keepdims=True)
        ot = jnp.dot(vt_ref[...], p_ref[g], preferred_element_type=F32)
        o_ref[pl.ds(q0, TQ_B), g * HD_B:(g + 1) * HD_B] = (ot / l).T.astype(BF16)

    def body(qt, mrun):
        for g in range(GQA_B):
            if g + 1 < GQA_B:
                nxt = score_pass(qt, g + 1)
            else:
                nxt = score_pass(jnp.minimum(qt + 1, nqt - 1), 0)
            value_pass(qt, g, mrun)
            mrun = nxt
        return mrun

    lax.fori_loop(0, nqt, body, score_pass(0, 0), unroll=GQA_UNROLL)


def _gqa(qb, kb, vt, side_weights=()):
    bsz, seq, _ = qb.shape
    gw = GQA_B * HD_B
    side_in, side_out, side_shapes = _side_cast_specs(side_weights, bsz * N_KV_B, lambda b, kv: b * N_KV_B + kv)
    return pl.pallas_call(
        functools.partial(_gqa_kernel, n_side=len(side_weights)),
        grid=(bsz, N_KV_B),
        in_specs=[pl.BlockSpec((None, seq, gw), lambda b, kv: (b, 0, kv)),
                  pl.BlockSpec((None, seq, HD_B), lambda b, kv: (b, 0, kv)),
                  pl.BlockSpec((None, HD_B, seq), lambda b, kv: (b, kv, 0))] + side_in,
        out_specs=[pl.BlockSpec((None, seq, gw), lambda b, kv: (b, 0, kv))] + side_out,
        out_shape=[jax.ShapeDtypeStruct((bsz, seq, N_HEADS_B * HD_B), BF16)] + side_shapes,
        scratch_shapes=[pltpu.VMEM((GQA_B, seq, TQ_B), F32), pltpu.VMEM((GQA_B, seq, TQ_B), BF16)],
        compiler_params=pltpu.CompilerParams(dimension_semantics=("arbitrary", "arbitrary"),
                                             vmem_limit_bytes=VMEM_LIMIT),
        name="gqa",
    )(qb, kb, vt, *side_weights)


def _merge_kernel(x_ref, gate_ref, ob_ref, o0_ref, s0_ref, o1_ref, s1_ref, o2_ref, s2_ref,
                  wa_ref, wb_ref, wo_ref, expand_ref, out_ref, oslab_ref, sslab_ref, tmp_ref):
    tm = x_ref.shape[0]
    nslab = GW_A // LANES

    def planes(g, src_o, src_s, dst_o, dst_s, lp):
        d = DILATED_GROUPS[g][1]
        d_dst = d // 4
        for r_dst in range(d_dst):
            for q in range(4):
                rows = pl.ds(r_dst * 4 * lp + q, lp, stride=4)
                dst_s(rows, src_s(r_dst + d_dst * q))
                for s in range(nslab):
                    dst_o(s, rows, src_o(r_dst + d_dst * q, s))

    lane_blk = lambda s: slice(s * LANES, (s + 1) * LANES)

    def put_tmp_o(s, rows, val):
        tmp_ref[s, rows, :] = val

    def put_tmp_s(rows, val):
        tmp_ref[nslab, rows, :] = val

    def put_slab_o(g):
        def put(s, rows, val):
            oslab_ref[g, s, rows, :] = val
        return put

    def put_slab_s(g):
        def put(rows, val):
            sslab_ref[g, rows, :] = val
        return put

    l1, l2 = tm // DILATED_GROUPS[1][1], tm // DILATED_GROUPS[2][1]
    planes(1, lambda r, s: o1_ref[r, :, lane_blk(s)].astype(F32), lambda r: s1_ref[r],
           put_slab_o(0), put_slab_s(0), l1)
    planes(2, lambda r, s: o2_ref[r, :, lane_blk(s)].astype(F32), lambda r: s2_ref[r],
           put_tmp_o, put_tmp_s, l2)
    planes(1, lambda r, s: tmp_ref[s, r * l1:(r + 1) * l1, :], lambda r: tmp_ref[nslab, r * l1:(r + 1) * l1, :],
           put_slab_o(1), put_slab_s(1), l1)

    stats = [s0_ref[...], sslab_ref[0], sslab_ref[1]]
    dens = [pltpu.roll(st, LANES - HEADS_A, axis=1) for st in stats]
    mx = jnp.maximum(jnp.maximum(stats[0], stats[1]), stats[2])
    es = [jnp.exp2(st - mx) for st in stats]
    total = es[0] * dens[0] + es[1] * dens[1] + es[2] * dens[2]
    head_lane = lax.broadcasted_iota(jnp.int32, (tm, LANES), 1) < HEADS_A
    o_a = jnp.zeros((tm, GW_A), F32)
    for g in range(N_GROUPS_A):
        w = jnp.where(head_lane, es[g] / total, 0.0)
        w_hi = w.astype(BF16)
        w_lo = (w - w_hi.astype(F32)).astype(BF16)
        wide = jnp.dot(jnp.concatenate([w_hi, w_lo], axis=1), expand_ref[...], preferred_element_type=F32)
        if g == 0:
            acc = o0_ref[...].astype(F32)
        else:
            acc = jnp.concatenate([oslab_ref[g - 1, s] for s in range(nslab)], axis=1)
        o_a = o_a + wide * acc
    o_a = o_a.astype(BF16)

    ya = jnp.dot(o_a, wa_ref[...], preferred_element_type=F32)
    yb = jnp.dot(ob_ref[...], wb_ref[...], preferred_element_type=F32)
    merged = (gate_ref[:, :D_MODEL].astype(F32) * ya + gate_ref[:, D_MODEL:].astype(F32) * yb).astype(BF16)
    out_ref[...] = x_ref[...] + jnp.dot(merged, wo_ref[...], preferred_element_type=F32)


def _merge(x1, gates, ob, outs, stats, wa, wb, wo):
    bsz, seq, _ = x1.shape
    tok = lambda w: pl.BlockSpec((None, TM, w), lambda b, t: (b, t, 0))
    a_specs = []
    for _, d in DILATED_GROUPS:
        for width in (GW_A, LANES):
            if d == 1:
                a_specs.append(pl.BlockSpec((None, None, TM, width), lambda b, t: (b, 0, t, 0)))
            else:
                a_specs.append(pl.BlockSpec((None, d, TM // d, width), lambda b, t: (b, 0, t, 0)))
    a_args = [arr for pair in zip(outs, stats) for arr in pair]
    src = jnp.arange(2 * LANES, dtype=jnp.int32)[:, None] % LANES
    dst = jnp.arange(GW_A, dtype=jnp.int32)[None, :] // HD_A
    expand = (src == dst).astype(BF16)
    return pl.pallas_call(
        _merge_kernel,
        grid=(bsz, seq // TM),
        in_specs=[tok(D_MODEL), tok(2 * D_MODEL), tok(N_HEADS_B * HD_B)] + a_specs
                 + [_resident((GW_A, D_MODEL)), _resident((N_HEADS_B * HD_B, D_MODEL)),
                    _resident((D_MODEL, D_MODEL)), _resident((2 * LANES, GW_A))],
        out_specs=tok(D_MODEL),
        out_shape=jax.ShapeDtypeStruct((bsz, seq, D_MODEL), F32),
        scratch_shapes=[pltpu.VMEM((2, GW_A // LANES, TM, LANES), F32),
                        pltpu.VMEM((2, TM, LANES), F32),
                        pltpu.VMEM((GW_A // LANES + 1, TM, LANES), F32)],
        compiler_params=pltpu.CompilerParams(dimension_semantics=("arbitrary", "arbitrary"),
                                             vmem_limit_bytes=VMEM_LIMIT),
        name="merge",
    )(x1, gates, ob, *a_args, wa, wb, wo, expand)


def _t5_bucket(rel):
    n = N_BUCKETS // 2
    max_exact = n // 2
    ret = jnp.where(rel > 0, n, 0)
    a = jnp.abs(rel)
    af = jnp.maximum(a, 1).astype(F32)
    large = max_exact + (jnp.log(af / max_exact) / math.log(MAX_DISTANCE / max_exact)
                         * (n - max_exact)).astype(jnp.int32)
    large = jnp.minimum(large, n - 1)
    return ret + jnp.where(a < max_exact, a, large)


def _bucket_tiles(dilation, half):
    qi = jnp.arange(QBLK_A, dtype=jnp.int32)[:, None]
    kj = jnp.arange(KWIN_A, dtype=jnp.int32)[None, :]
    tiles = []
    for shift in (0, (KWIN_A - QBLK_A) // 2, KWIN_A - QBLK_A):
        rel = kj - shift - qi
        tiles.append(jnp.where(jnp.abs(rel) <= half, _t5_bucket(rel * dilation), -1))
    return jnp.stack(tiles)


def _rope_tables(seq):
    t = np.arange(seq)
    n_freq = HD_B // 4
    freq = (ROPE_THETA ** (-np.arange(n_freq, dtype=np.float32) / n_freq)).astype(np.float32)
    ang = np.concatenate([(t // GRID_W).astype(np.float32)[:, None] * freq,
                          (t % GRID_W).astype(np.float32)[:, None] * freq], -1)
    cos, sin = np.cos(ang), np.sin(ang)
    cosf = np.repeat(cos, 2, axis=-1)
    sinf = np.stack([-sin, sin], axis=-1).reshape(seq, HD_B)
    return jnp.asarray(cosf, F32), jnp.asarray(sinf, F32)


def kernel(x, ffn1_norm, ffn1_w1, ffn1_w3, ffn1_w2, mix_norm, w_in, b_gate, q_norm, k_norm, rel_bias,
           w_branch_a, w_branch_b, w_out, ffn2_norm, ffn2_w1, ffn2_w3, ffn2_w2, final_norm):
    bsz, seq, dm = x.shape
    assert dm == D_MODEL and seq % TM == 0 and ffn1_w1.shape[0] == 1
    row = lambda v: v.reshape(1, -1).astype(F32)
    cosf, sinf = _rope_tables(seq)
    qn = row(q_norm[0] * (HD_B ** -0.5 * LOG2E))
    kn = row(k_norm[0])

    x1, w_in_bf = _ffn(x.reshape(bsz * seq, dm), row(ffn1_norm[0]), ffn1_w1[0], ffn1_w3[0], ffn1_w2[0],
                       row(final_norm), False, side_weights=(w_in[0],))
    x1 = x1.reshape(bsz, seq, dm)
    a0, a1, a2, qb, kb, vt, gates = _proj(x1, row(mix_norm[0]), w_in_bf, row(b_gate[0]), qn, kn, cosf, sinf)

    outs, stats = [], []
    for g, (a_g, (window, d)) in enumerate(zip((a0, a1, a2), DILATED_GROUPS)):
        table = rel_bias[:, g * HEADS_A:(g + 1) * HEADS_A].astype(F32)
        o_g, st_g = _dilated(a_g, _bucket_tiles(d, window // (2 * d)), table, g)
        outs.append(o_g)
        stats.append(st_g)
    ob, wa, wb, wo, w1, w3, w2 = _gqa(qb, kb, vt, side_weights=(
        w_branch_a[0], w_branch_b[0], w_out[0], ffn2_w1[0], ffn2_w3[0], ffn2_w2[0]))

    x2 = _merge(x1, gates, ob, outs, stats, wa, wb, wo)
    y, = _ffn(x2.reshape(bsz * seq, dm), row(ffn2_norm[0]), w1, w3, w2, row(final_norm), True)
    return y.reshape(bsz, seq, dm)
```

```python
import functools
import math

import jax
import jax.numpy as jnp
import numpy as np
from jax import lax
from jax.experimental import pallas as pl
from jax.experimental.pallas import tpu as pltpu

D_MODEL = 1024
D_FF = 2816
DILATED_GROUPS = ((128, 1), (512, 4), (2048, 16))
N_GROUPS_A = 3
HEADS_A = 8
HD_A = 64
GW_A = HEADS_A * HD_A
N_HEADS_B = 8
N_KV_B = 2
GQA_B = N_HEADS_B // N_KV_B
HD_B = 128
GRID_W = 64
ROPE_THETA = 10000.0
N_BUCKETS = 32
MAX_DISTANCE = 1024
EPS = 1e-6
NEG_INF = -1e30
LOG2E = 1.4426950408889634

LANES = 128
F32_SUBLANES = 8
BF16_SUBLANES = 16
TM = 512
TM_FFN = 1024
FFN_CHUNK = 256
PROJ_CHUNK = 256
QBLK_A = 128
KWIN_A = 256
DILATED_ROWS = 4096
DILATED_UNROLL = 16
TQ_B = 256
KCH_B = 512
GQA_UNROLL = 4
VMEM_LIMIT = 60 * 1024 * 1024
WEIGHT_STAGE_BYTES = 2 * 1024 * 1024
WEIGHT_STAGE_SLOTS = 4

BF16 = jnp.bfloat16
F32 = jnp.float32


_HBM = pl.BlockSpec(memory_space=pl.ANY)


def _resident(shape):
    nd = len(shape)
    return pl.BlockSpec(shape, lambda *_: (0,) * nd, pipeline_mode=pl.Buffered(1))


def _rms(x, gain):
    ms = jnp.mean(x * x, axis=-1, keepdims=True)
    return x * lax.rsqrt(ms + EPS) * gain


def _load_weight_bf16(src_hbm, dst_ref):
    rows, cols = src_hbm.shape
    chunk_rows = max(r for r in range(BF16_SUBLANES, rows + 1, BF16_SUBLANES)
                     if rows % r == 0 and r * cols * 4 <= WEIGHT_STAGE_BYTES)
    nchunk = rows // chunk_rows

    ahead = WEIGHT_STAGE_SLOTS - 1

    def run(stage_ref, sem_ref):
        def copy(i):
            slot = i % WEIGHT_STAGE_SLOTS
            return pltpu.make_async_copy(src_hbm.at[pl.ds(i * chunk_rows, chunk_rows), :],
                                         stage_ref.at[slot], sem_ref.at[slot])
        for i in range(min(ahead, nchunk)):
            copy(i).start()
        for i in range(nchunk):
            if i + ahead < nchunk:
                copy(i + ahead).start()
            copy(i).wait()
            dst_ref[i * chunk_rows:(i + 1) * chunk_rows, :] = stage_ref[i % WEIGHT_STAGE_SLOTS].astype(BF16)

    pl.run_scoped(run, pltpu.VMEM((WEIGHT_STAGE_SLOTS, chunk_rows, cols), F32),
                  pltpu.SemaphoreType.DMA((WEIGHT_STAGE_SLOTS,)))


def _side_cast_specs(weights, nsteps, step_of):
    in_specs, out_specs, out_shapes = [], [], []
    for w in weights:
        rows, cols = w.shape
        assert rows % (BF16_SUBLANES * nsteps) == 0, (w.shape, nsteps)
        index = lambda *ids: (step_of(*ids), 0)
        in_specs.append(pl.BlockSpec((rows // nsteps, cols), index))
        out_specs.append(pl.BlockSpec((rows // nsteps, cols), index))
        out_shapes.append(jax.ShapeDtypeStruct((rows, cols), BF16))
    return in_specs, out_specs, out_shapes


def _side_casts(side_in, side_out):
    for src, dst in zip(side_in, side_out, strict=True):
        dst[...] = src[...].astype(BF16)


def _ffn_kernel(x_ref, g_ref, w1_in, w3_in, w2_in, fg_ref, *rest, final_norm, n_side):
    side_in, (o_ref, *side_out), scratch = rest[:n_side], rest[n_side:2 * n_side + 1], rest[2 * n_side + 1:]
    _side_casts(side_in, side_out)
    if scratch:
        w1_ref, w3_ref, w2_ref = scratch

        @pl.when(pl.program_id(0) == 0)
        def _load_weights():
            _load_weight_bf16(w1_in, w1_ref)
            _load_weight_bf16(w3_in, w3_ref)
            _load_weight_bf16(w2_in, w2_ref)
    else:
        w1_ref, w3_ref, w2_ref = w1_in, w3_in, w2_in

    x = x_ref[...]
    h = _rms(x, g_ref[...]).astype(BF16)
    acc = jnp.zeros((x.shape[0], D_MODEL), F32)
    for j in range(D_FF // FFN_CHUNK):
        c = slice(j * FFN_CHUNK, (j + 1) * FFN_CHUNK)
        a = jnp.dot(h, w1_ref[:, c], preferred_element_type=F32)
        b = jnp.dot(h, w3_ref[:, c], preferred_element_type=F32)
        g = (a * jax.nn.sigmoid(a) * b).astype(BF16)
        acc = acc + jnp.dot(g, w2_ref[c, :], preferred_element_type=F32)
    y = x + 0.5 * acc
    if final_norm:
        y = _rms(y, fg_ref[...])
    o_ref[...] = y


def _ffn(x2d, gain, w1, w3, w2, final_gain, final_norm, side_weights=()):
    n = x2d.shape[0]
    assert n % TM_FFN == 0
    nsteps = n // TM_FFN
    row = pl.BlockSpec((TM_FFN, D_MODEL), lambda i: (i, 0))
    side_in, side_out, side_shapes = _side_cast_specs(side_weights, nsteps, lambda i: i)
    if w1.dtype == BF16:
        w_specs = [_resident(w1.shape), _resident(w3.shape), _resident(w2.shape)]
        scratch = []
    else:
        w_specs = [_HBM, _HBM, _HBM]
        scratch = [pltpu.VMEM(w1.shape, BF16), pltpu.VMEM(w3.shape, BF16), pltpu.VMEM(w2.shape, BF16)]
    return pl.pallas_call(
        functools.partial(_ffn_kernel, final_norm=final_norm, n_side=len(side_weights)),
        grid=(nsteps,),
        in_specs=[row, _resident((1, D_MODEL))] + w_specs + [_resident((1, D_MODEL))] + side_in,
        out_specs=[row] + side_out,
        out_shape=[jax.ShapeDtypeStruct((n, D_MODEL), F32)] + side_shapes,
        scratch_shapes=scratch,
        compiler_params=pltpu.CompilerParams(dimension_semantics=("arbitrary",), vmem_limit_bytes=VMEM_LIMIT),
        name="ffn_final" if final_norm else "ffn",
    )(x2d, gain, w1, w3, w2, final_gain, *side_weights)


A_WIDTH = 3 * GW_A
QSCALE_A = HD_A ** -0.5 * LOG2E
OFF_QB = N_GROUPS_A * A_WIDTH
OFF_KB = OFF_QB + N_HEADS_B * HD_B
OFF_VB = OFF_KB + N_KV_B * HD_B
OFF_GATE = OFF_VB + N_KV_B * HD_B
IN_WIDTH = OFF_GATE + 2 * D_MODEL


def _proj_kernel(x_ref, g_ref, w_ref, bg_ref, qn_ref, kn_ref, cos_ref, sin_ref,
                 a0_ref, a1_ref, a2_ref, qb_ref, kb_ref, vt_ref, gate_ref, hs_ref, hb_ref):
    tm = x_ref.shape[0]
    nslab = D_MODEL // LANES
    even_lane = (lax.broadcasted_iota(jnp.int32, (tm, HD_B), 1) & 1) == 0

    hn = _rms(x_ref[...], g_ref[...])
    hb_ref[0] = hn.astype(BF16)
    for s in range(nslab):
        hs_ref[0, s] = hn[:, s * LANES:(s + 1) * LANES]
    d_prev = 1
    for g in range(1, N_GROUPS_A):
        d = DILATED_GROUPS[g][1]
        f, lp, ln = d // d_prev, tm // d_prev, tm // d
        for r_prev in range(d_prev):
            for q in range(f):
                r = r_prev + d_prev * q
                for s in range(nslab):
                    rows = hs_ref[(g - 1) % 2, s, pl.ds(r_prev * lp + q, ln, stride=f), :]
                    hb_ref[g, r * ln:(r + 1) * ln, s * LANES:(s + 1) * LANES] = rows.astype(BF16)
                    if g + 1 < N_GROUPS_A:
                        hs_ref[g % 2, s, r * ln:(r + 1) * ln, :] = rows
        d_prev = d

    def mm(lo, g=0):
        return jnp.dot(hb_ref[g], w_ref[:, lo:lo + PROJ_CHUNK], preferred_element_type=F32)

    def _rope(y, cosf, sinf):
        partner = jnp.where(even_lane, pltpu.roll(y, HD_B - 1, axis=1), pltpu.roll(y, 1, axis=1))
        return y * cosf + partner * sinf

    def a_chunk(g, j):
        part, half = divmod(j, GW_A // PROJ_CHUNK)
        p = mm((part * N_GROUPS_A + g) * GW_A + half * PROJ_CHUNK, g)
        return p * QSCALE_A if part == 0 else p

    for j in range(2 * D_MODEL // PROJ_CHUNK):
        c = slice(j * PROJ_CHUNK, (j + 1) * PROJ_CHUNK)
        pg = mm(OFF_GATE + j * PROJ_CHUNK) + bg_ref[:, c]
        gate_ref[:, c] = (0.5 * jnp.tanh(0.5 * pg) + 0.5).astype(BF16)

    cosf = cos_ref[...]
    sinf = sin_ref[...]
    heads_per_chunk = PROJ_CHUNK // HD_B
    for j in range(N_HEADS_B // heads_per_chunk):
        pq = mm(OFF_QB + j * PROJ_CHUNK)
        for hh in range(heads_per_chunk):
            src = slice(hh * HD_B, (hh + 1) * HD_B)
            dst = slice(j * PROJ_CHUNK + hh * HD_B, j * PROJ_CHUNK + (hh + 1) * HD_B)
            qb_ref[:, dst] = _rope(_rms(pq[:, src], qn_ref[...]), cosf, sinf).astype(BF16)
    pk = mm(OFF_KB)
    for hh in range(N_KV_B):
        c = slice(hh * HD_B, (hh + 1) * HD_B)
        kb_ref[:, c] = _rope(_rms(pk[:, c], kn_ref[...]), cosf, sinf).astype(BF16)
    vt_ref[...] = mm(OFF_VB).T.astype(BF16)

    for g, a_ref in ((2, a2_ref), (1, a1_ref)):
        d = DILATED_GROUPS[g][1]
        for j in range(A_WIDTH // PROJ_CHUNK):
            c = slice(j * PROJ_CHUNK, (j + 1) * PROJ_CHUNK)
            a_ref[:, :, c] = a_chunk(g, j).reshape(d, tm // d, PROJ_CHUNK).astype(BF16)
    for j in range(A_WIDTH // PROJ_CHUNK):
        c = slice(j * PROJ_CHUNK, (j + 1) * PROJ_CHUNK)
        a0_ref[:, c] = a_chunk(0, j).astype(BF16)


def _proj(x1, gain, w_in, b_gate, qn, kn, cosf, sinf):
    bsz, seq, _ = x1.shape
    nt = seq // TM
    outs, specs = [], []
    for _, d in DILATED_GROUPS:
        outs.append(jax.ShapeDtypeStruct((bsz, d, seq // d, A_WIDTH), BF16))
        specs.append(pl.BlockSpec((None, d, TM // d, A_WIDTH), lambda b, t: (b, 0, t, 0)))
    outs += [jax.ShapeDtypeStruct((bsz, seq, N_HEADS_B * HD_B), BF16),
             jax.ShapeDtypeStruct((bsz, seq, N_KV_B * HD_B), BF16),
             jax.ShapeDtypeStruct((bsz, N_KV_B * HD_B, seq), BF16),
             jax.ShapeDtypeStruct((bsz, seq, 2 * D_MODEL), BF16)]
    specs += [pl.BlockSpec((None, TM, N_HEADS_B * HD_B), lambda b, t: (b, t, 0)),
              pl.BlockSpec((None, TM, N_KV_B * HD_B), lambda b, t: (b, t, 0)),
              pl.BlockSpec((None, N_KV_B * HD_B, TM), lambda b, t: (b, 0, t)),
              pl.BlockSpec((None, TM, 2 * D_MODEL), lambda b, t: (b, t, 0))]
    specs[0] = pl.BlockSpec((None, None, TM, A_WIDTH), lambda b, t: (b, 0, t, 0))
    return pl.pallas_call(
        _proj_kernel,
        grid=(bsz, nt),
        in_specs=[pl.BlockSpec((None, TM, D_MODEL), lambda b, t: (b, t, 0)),
                  _resident((1, D_MODEL)), _resident((D_MODEL, IN_WIDTH)), _resident((1, 2 * D_MODEL)),
                  _resident((1, HD_B)), _resident((1, HD_B)),
                  pl.BlockSpec((TM, HD_B), lambda b, t: (t, 0)),
                  pl.BlockSpec((TM, HD_B), lambda b, t: (t, 0))],
        out_specs=specs,
        out_shape=outs,
        scratch_shapes=[pltpu.VMEM((2, D_MODEL // LANES, TM, LANES), F32),
                        pltpu.VMEM((N_GROUPS_A, TM, D_MODEL), BF16)],
        compiler_params=pltpu.CompilerParams(dimension_semantics=("arbitrary", "arbitrary"),
                                             vmem_limit_bytes=VMEM_LIMIT),
        name="proj",
    )(x1, gain, w_in, b_gate, qn, kn, cosf, sinf)


def _dilated_kernel(bucket_ref, tab_ref, q_ref, k_ref, v_ref, o_ref, st_ref, bias_ref, s_ref, *, sub_len):
    nblk = sub_len // QBLK_A
    nunit = q_ref.shape[0] * nblk
    npair = HEADS_A // 2

    @pl.when((pl.program_id(0) == 0) & (pl.program_id(1) == 0))
    def _build_bias():
        for v in range(3):
            bt = bucket_ref[v]
            accs = [jnp.full((QBLK_A, KWIN_A), NEG_INF, F32)] * HEADS_A
            for j in range(N_BUCKETS):
                hit = bt == j
                accs = [jnp.where(hit, tab_ref[j, hh] * LOG2E, acc) for hh, acc in enumerate(accs)]
            for hh, acc in enumerate(accs):
                bias_ref[v, hh // 2, (hh % 2) * QBLK_A:(hh % 2 + 1) * QBLK_A, :] = acc

    lane = lax.broadcasted_iota(jnp.int32, (QBLK_A, LANES), 1)
    low = lane < HD_A

    def geometry(u):
        r = u >> (nblk.bit_length() - 1)
        n = u & (nblk - 1)
        q0 = pl.multiple_of(n * QBLK_A, QBLK_A)
        w0 = pl.multiple_of(jnp.clip(n * QBLK_A - (KWIN_A - QBLK_A) // 2, 0, sub_len - KWIN_A), 64)
        variant = jnp.where(n == 0, 0, jnp.where(n == nblk - 1, 2, 1))
        return r, q0, w0, variant

    def score_pass(u, pair):
        r, q0, w0, variant = geometry(u)
        c = slice(pair * LANES, (pair + 1) * LANES)
        qp = q_ref[r, pl.ds(q0, QBLK_A), c]
        kp = k_ref[r, pl.ds(w0, KWIN_A), c]
        zero = jnp.zeros_like(qp)
        q2 = jnp.concatenate([jnp.where(low, qp, zero), jnp.where(low, zero, qp)], axis=0)
        s = lax.dot_general(q2, kp, (((1,), (1,)), ((), ())), preferred_element_type=F32)
        s = s + bias_ref[variant, pair]
        s_ref[pair] = s
        return jnp.max(s, axis=-1, keepdims=True)

    def value_pass(u, pair, m, stats):
        r, q0, w0, _ = geometry(u)
        c = slice(pair * LANES, (pair + 1) * LANES)
        vp = v_ref[r, pl.ds(w0, KWIN_A), c]
        p = jnp.exp2(s_ref[pair] - m)
        l = jnp.sum(p, axis=-1, keepdims=True)
        o = jnp.dot(p.astype(BF16), vp, preferred_element_type=F32)
        o_ref[r, pl.ds(q0, QBLK_A), c] = jnp.where(low, o[:QBLK_A], o[QBLK_A:]).astype(BF16)
        for half in range(2):
            rows = slice(half * QBLK_A, (half + 1) * QBLK_A)
            hh = 2 * pair + half
            stats = jnp.where(lane == hh, m[rows], stats)
            stats = jnp.where(lane == HEADS_A + hh, l[rows], stats)
        return stats

    def body(u, ms):
        stats = jnp.zeros((QBLK_A, LANES), F32)
        for pair in range(npair):
            if pair + 1 < npair:
                nxt = score_pass(u, pair + 1)
            else:
                nxt = score_pass(jnp.minimum(u + 1, nunit - 1), 0)
            stats = value_pass(u, pair, ms, stats)
            ms = nxt
        r, q0, _, _ = geometry(u)
        st_ref[r, pl.ds(q0, QBLK_A), :] = stats
        return ms

    lax.fori_loop(0, nunit, body, score_pass(0, 0), unroll=DILATED_UNROLL)


def _dilated(a_g, buckets, table, g):
    bsz, d, sub_len, _ = a_g.shape
    nblk = sub_len // QBLK_A
    assert sub_len % QBLK_A == 0 and sub_len >= KWIN_A and nblk & (nblk - 1) == 0
    rb = min(d, max(1, DILATED_ROWS // sub_len))
    assert d % rb == 0
    blk = lambda col: pl.BlockSpec((None, rb, sub_len, GW_A), lambda b, r: (b, r, 0, col))
    return pl.pallas_call(
        functools.partial(_dilated_kernel, sub_len=sub_len),
        grid=(bsz, d // rb),
        in_specs=[_resident((3, QBLK_A, KWIN_A)),
                  pl.BlockSpec(memory_space=pltpu.SMEM),
                  blk(0), blk(1), blk(2)],
        out_specs=[blk(0), pl.BlockSpec((None, rb, sub_len, LANES), lambda b, r: (b, r, 0, 0))],
        out_shape=[jax.ShapeDtypeStruct((bsz, d, sub_len, GW_A), BF16),
                   jax.ShapeDtypeStruct((bsz, d, sub_len, LANES), F32)],
        scratch_shapes=[pltpu.VMEM((3, HEADS_A // 2, 2 * QBLK_A, KWIN_A), F32),
                        pltpu.VMEM((HEADS_A // 2, 2 * QBLK_A, KWIN_A), F32)],
        compiler_params=pltpu.CompilerParams(dimension_semantics=("arbitrary", "arbitrary"),
                                             vmem_limit_bytes=VMEM_LIMIT),
        name=f"dilated{g}",
    )(buckets, table, a_g, a_g, a_g)


def _gqa_kernel(q_ref, k_ref, vt_ref, *rest, n_side):
    side_in, (o_ref, *side_out), (s_ref, p_ref) = rest[:n_side], rest[n_side:2 * n_side + 1], rest[2 * n_side + 1:]
    _side_casts(side_in, side_out)
    seq = k_ref.shape[0]
    nch = seq // KCH_B
    nqt = seq // TQ_B

    def score_pass(qt, g):
        q0 = pl.multiple_of(qt * TQ_B, TQ_B)
        q = q_ref[pl.ds(q0, TQ_B), g * HD_B:(g + 1) * HD_B]
        mrun = jnp.full((F32_SUBLANES, TQ_B), NEG_INF, F32)
        for ch in range(nch):
            rows = slice(ch * KCH_B, (ch + 1) * KCH_B)
            st = lax.dot_general(k_ref[rows, :], q, (((1,), (1,)), ((), ())),
                                 preferred_element_type=F32)
            s_ref[g, rows, :] = st
            mrun = jnp.maximum(mrun, jnp.max(st.reshape(KCH_B // F32_SUBLANES, F32_SUBLANES, TQ_B), axis=0))
        return mrun

    def value_pass(qt, g, mrun):
        q0 = pl.multiple_of(qt * TQ_B, TQ_B)
        m = jnp.max(mrun, axis=0, keepdims=True)
        lrun = jnp.zeros((F32_SUBLANES, TQ_B), F32)
        for ch in range(nch):
            rows = slice(ch * KCH_B, (ch + 1) * KCH_B)
            p = jnp.exp2(s_ref[g, rows, :] - m)
            lrun = lrun + jnp.sum(p.reshape(KCH_B // F32_SUBLANES, F32_SUBLANES, TQ_B), axis=0)
            p_ref[g, rows, :] = p.astype(BF16)
        l = jnp.sum(lrun, axis=0, keepdims=True)
        ot = jnp.dot(vt_ref[...], p_ref[g], preferred_element_type=F32)
        o_ref[pl.ds(q0, TQ_B), g * HD_B:(g + 1) * HD_B] = (ot / l).T.astype(BF16)

    def body(qt, mrun):
        for g in range(GQA_B):
            if g + 1 < GQA_B:
                nxt = score_pass(qt, g + 1)
            else:
                nxt = score_pass(jnp.minimum(qt + 1, nqt - 1), 0)
            value_pass(qt, g, mrun)
            mrun = nxt
        return mrun

    lax.fori_loop(0, nqt, body, score_pass(0, 0), unroll=GQA_UNROLL)


def _gqa(qb, kb, vt, side_weights=()):
    bsz, seq, _ = qb.shape
    gw = GQA_B * HD_B
    side_in, side_out, side_shapes = _side_cast_specs(side_weights, bsz * N_KV_B, lambda b, kv: b * N_KV_B + kv)
    return pl.pallas_call(
        functools.partial(_gqa_kernel, n_side=len(side_weights)),
        grid=(bsz, N_KV_B),
        in_specs=[pl.BlockSpec((None, seq, gw), lambda b, kv: (b, 0, kv)),
                  pl.BlockSpec((None, seq, HD_B), lambda b, kv: (b, 0, kv)),
                  pl.BlockSpec((None, HD_B, seq), lambda b, kv: (b, kv, 0))] + side_in,
        out_specs=[pl.BlockSpec((None, seq, gw), lambda b, kv: (b, 0, kv))] + side_out,
        out_shape=[jax.ShapeDtypeStruct((bsz, seq, N_HEADS_B * HD_B), BF16)] + side_shapes,
        scratch_shapes=[pltpu.VMEM((GQA_B, seq, TQ_B), F32), pltpu.VMEM((GQA_B, seq, TQ_B), BF16)],
        compiler_params=pltpu.CompilerParams(dimension_semantics=("arbitrary", "arbitrary"),
                                             vmem_limit_bytes=VMEM_LIMIT),
        name="gqa",
    )(qb, kb, vt, *side_weights)


def _merge_kernel(x_ref, gate_ref, ob_ref, o0_ref, s0_ref, o1_ref, s1_ref, o2_ref, s2_ref,
                  wa_ref, wb_ref, wo_ref, expand_ref, out_ref, oslab_ref, sslab_ref, tmp_ref):
    tm = x_ref.shape[0]
    nslab = GW_A // LANES

    def planes(g, src_o, src_s, dst_o, dst_s, lp):
        d = DILATED_GROUPS[g][1]
        d_dst = d // 4
        for r_dst in range(d_dst):
            for q in range(4):
                rows = pl.ds(r_dst * 4 * lp + q, lp, stride=4)
                dst_s(rows, src_s(r_dst + d_dst * q))
                for s in range(nslab):
                    dst_o(s, rows, src_o(r_dst + d_dst * q, s))

    lane_blk = lambda s: slice(s * LANES, (s + 1) * LANES)

    def put_tmp_o(s, rows, val):
        tmp_ref[s, rows, :] = val

    def put_tmp_s(rows, val):
        tmp_ref[nslab, rows, :] = val

    def put_slab_o(g):
        def put(s, rows, val):
            oslab_ref[g, s, rows, :] = val
        return put

    def put_slab_s(g):
        def put(rows, val):
            sslab_ref[g, rows, :] = val
        return put

    l1, l2 = tm // DILATED_GROUPS[1][1], tm // DILATED_GROUPS[2][1]
    planes(1, lambda r, s: o1_ref[r, :, lane_blk(s)].astype(F32), lambda r: s1_ref[r],
           put_slab_o(0), put_slab_s(0), l1)
    planes(2, lambda r, s: o2_ref[r, :, lane_blk(s)].astype(F32), lambda r: s2_ref[r],
           put_tmp_o, put_tmp_s, l2)
    planes(1, lambda r, s: tmp_ref[s, r * l1:(r + 1) * l1, :], lambda r: tmp_ref[nslab, r * l1:(r + 1) * l1, :],
           put_slab_o(1), put_slab_s(1), l1)

    stats = [s0_ref[...], sslab_ref[0], sslab_ref[1]]
    dens = [pltpu.roll(st, LANES - HEADS_A, axis=1) for st in stats]
    mx = jnp.maximum(jnp.maximum(stats[0], stats[1]), stats[2])
    es = [jnp.exp2(st - mx) for st in stats]
    total = es[0] * dens[0] + es[1] * dens[1] + es[2] * dens[2]
    head_lane = lax.broadcasted_iota(jnp.int32, (tm, LANES), 1) < HEADS_A
    o_a = jnp.zeros((tm, GW_A), F32)
    for g in range(N_GROUPS_A):
        w = jnp.where(head_lane, es[g] / total, 0.0)
        w_hi = w.astype(BF16)
        w_lo = (w - w_hi.astype(F32)).astype(BF16)
        wide = jnp.dot(jnp.concatenate([w_hi, w_lo], axis=1), expand_ref[...], preferred_element_type=F32)
        if g == 0:
            acc = o0_ref[...].astype(F32)
        else:
            acc = jnp.concatenate([oslab_ref[g - 1, s] for s in range(nslab)], axis=1)
        o_a = o_a + wide * acc
    o_a = o_a.astype(BF16)

    ya = jnp.dot(o_a, wa_ref[...], preferred_element_type=F32)
    yb = jnp.dot(ob_ref[...], wb_ref[...], preferred_element_type=F32)
    merged = (gate_ref[:, :D_MODEL].astype(F32) * ya + gate_ref[:, D_MODEL:].astype(F32) * yb).astype(BF16)
    out_ref[...] = x_ref[...] + jnp.dot(merged, wo_ref[...], preferred_element_type=F32)


def _merge(x1, gates, ob, outs, stats, wa, wb, wo):
    bsz, seq, _ = x1.shape
    tok = lambda w: pl.BlockSpec((None, TM, w), lambda b, t: (b, t, 0))
    a_specs = []
    for _, d in DILATED_GROUPS:
        for width in (GW_A, LANES):
            if d == 1:
                a_specs.append(pl.BlockSpec((None, None, TM, width), lambda b, t: (b, 0, t, 0)))
            else:
                a_specs.append(pl.BlockSpec((None, d, TM // d, width), lambda b, t: (b, 0, t, 0)))
    a_args = [arr for pair in zip(outs, stats) for arr in pair]
    src = jnp.arange(2 * LANES, dtype=jnp.int32)[:, None] % LANES
    dst = jnp.arange(GW_A, dtype=jnp.int32)[None, :] // HD_A
    expand = (src == dst).astype(BF16)
    return pl.pallas_call(
        _merge_kernel,
        grid=(bsz, seq // TM),
        in_specs=[tok(D_MODEL), tok(2 * D_MODEL), tok(N_HEADS_B * HD_B)] + a_specs
                 + [_resident((GW_A, D_MODEL)), _resident((N_HEADS_B * HD_B, D_MODEL)),
                    _resident((D_MODEL, D_MODEL)), _resident((2 * LANES, GW_A))],
        out_specs=tok(D_MODEL),
        out_shape=jax.ShapeDtypeStruct((bsz, seq, D_MODEL), F32),
        scratch_shapes=[pltpu.VMEM((2, GW_A // LANES, TM, LANES), F32),
                        pltpu.VMEM((2, TM, LANES), F32),
                        pltpu.VMEM((GW_A // LANES + 1, TM, LANES), F32)],
        compiler_params=pltpu.CompilerParams(dimension_semantics=("arbitrary", "arbitrary"),
                                             vmem_limit_bytes=VMEM_LIMIT),
        name="merge",
    )(x1, gates, ob, *a_args, wa, wb, wo, expand)


def _t5_bucket(rel):
    n = N_BUCKETS // 2
    max_exact = n // 2
    ret = jnp.where(rel > 0, n, 0)
    a = jnp.abs(rel)
    af = jnp.maximum(a, 1).astype(F32)
    large = max_exact + (jnp.log(af / max_exact) / math.log(MAX_DISTANCE / max_exact)
                         * (n - max_exact)).astype(jnp.int32)
    large = jnp.minimum(large, n - 1)
    return ret + jnp.where(a < max_exact, a, large)


def _bucket_tiles(dilation, half):
    qi = jnp.arange(QBLK_A, dtype=jnp.int32)[:, None]
    kj = jnp.arange(KWIN_A, dtype=jnp.int32)[None, :]
    tiles = []
    for shift in (0, (KWIN_A - QBLK_A) // 2, KWIN_A - QBLK_A):
        rel = kj - shift - qi
        tiles.append(jnp.where(jnp.abs(rel) <= half, _t5_bucket(rel * dilation), -1))
    return jnp.stack(tiles)


def _rope_tables(seq):
    t = np.arange(seq)
    n_freq = HD_B // 4
    freq = (ROPE_THETA ** (-np.arange(n_freq, dtype=np.float32) / n_freq)).astype(np.float32)
    ang = np.concatenate([(t // GRID_W).astype(np.float32)[:, None] * freq,
                          (t % GRID_W).astype(np.float32)[:, None] * freq], -1)
    cos, sin = np.cos(ang), np.sin(ang)
    cosf = np.repeat(cos, 2, axis=-1)
    sinf = np.stack([-sin, sin], axis=-1).reshape(seq, HD_B)
    return jnp.asarray(cosf, F32), jnp.asarray(sinf, F32)


def kernel(x, ffn1_norm, ffn1_w1, ffn1_w3, ffn1_w2, mix_norm, w_in, b_gate, q_norm, k_norm, rel_bias,
           w_branch_a, w_branch_b, w_out, ffn2_norm, ffn2_w1, ffn2_w3, ffn2_w2, final_norm):
    bsz, seq, dm = x.shape
    assert dm == D_MODEL and seq % TM == 0 and ffn1_w1.shape[0] == 1
    row = lambda v: v.reshape(1, -1).astype(F32)
    cosf, sinf = _rope_tables(seq)
    qn = row(q_norm[0] * (HD_B ** -0.5 * LOG2E))
    kn = row(k_norm[0])

    x1, w_in_bf = _ffn(x.reshape(bsz * seq, dm), row(ffn1_norm[0]), ffn1_w1[0], ffn1_w3[0], ffn1_w2[0],
                       row(final_norm), False, side_weights=(w_in[0],))
    x1 = x1.reshape(bsz, seq, dm)
    a0, a1, a2, qb, kb, vt, gates = _proj(x1, row(mix_norm[0]), w_in_bf, row(b_gate[0]), qn, kn, cosf, sinf)

    outs, stats = [], []
    for g, (a_g, (window, d)) in enumerate(zip((a0, a1, a2), DILATED_GROUPS)):
        table = rel_bias[:, g * HEADS_A:(g + 1) * HEADS_A].astype(F32)
        o_g, st_g = _dilated(a_g, _bucket_tiles(d, window // (2 * d)), table, g)
        outs.append(o_g)
        stats.append(st_g)
    ob, wa, wb, wo, w1, w3, w2 = _gqa(qb, kb, vt, side_weights=(
        w_branch_a[0], w_branch_b[0], w_out[0], ffn2_w1[0], ffn2_w3[0], ffn2_w2[0]))

    x2 = _merge(x1, gates, ob, outs, stats, wa, wb, wo)
    y, = _ffn(x2.reshape(bsz * seq, dm), row(ffn2_norm[0]), w1, w3, w2, row(final_norm), True)
    return y.reshape(bsz, seq, dm)
```

```python
import functools
import math

import jax
import jax.numpy as jnp
import numpy as np
from jax import lax
from jax.experimental import pallas as pl
from jax.experimental.pallas import tpu as pltpu

D_MODEL = 1024
D_FF = 2816
DILATED_GROUPS = ((128, 1), (512, 4), (2048, 16))
N_GROUPS_A = 3
HEADS_A = 8
HD_A = 64
GW_A = HEADS_A * HD_A
N_HEADS_B = 8
N_KV_B = 2
GQA_B = N_HEADS_B // N_KV_B
HD_B = 128
GRID_W = 64
ROPE_THETA = 10000.0
N_BUCKETS = 32
MAX_DISTANCE = 1024
EPS = 1e-6
NEG_INF = -1e30
LOG2E = 1.4426950408889634

LANES = 128
F32_SUBLANES = 8
BF16_SUBLANES = 16
TM = 512
TM_FFN = 1024
FFN_CHUNK = 256
PROJ_CHUNK = 256
QBLK_A = 128
KWIN_A = 256
DILATED_ROWS = 4096
DILATED_UNROLL = 16
TQ_B = 256
KCH_B = 512
GQA_UNROLL = 4
VMEM_LIMIT = 60 * 1024 * 1024
WEIGHT_STAGE_BYTES = 2 * 1024 * 1024
WEIGHT_STAGE_SLOTS = 4

BF16 = jnp.bfloat16
F32 = jnp.float32


_HBM = pl.BlockSpec(memory_space=pl.ANY)


def _resident(shape):
    nd = len(shape)
    return pl.BlockSpec(shape, lambda *_: (0,) * nd, pipeline_mode=pl.Buffered(1))


def _rms(x, gain):
    ms = jnp.mean(x * x, axis=-1, keepdims=True)
    return x * lax.rsqrt(ms + EPS) * gain


def _load_weight_bf16(src_hbm, dst_ref):
    rows, cols = src_hbm.shape
    chunk_rows = max(r for r in range(BF16_SUBLANES, rows + 1, BF16_SUBLANES)
                     if rows % r == 0 and r * cols * 4 <= WEIGHT_STAGE_BYTES)
    nchunk = rows // chunk_rows

    ahead = WEIGHT_STAGE_SLOTS - 1

    def run(stage_ref, sem_ref):
        def copy(i):
            slot = i % WEIGHT_STAGE_SLOTS
            return pltpu.make_async_copy(src_hbm.at[pl.ds(i * chunk_rows, chunk_rows), :],
                                         stage_ref.at[slot], sem_ref.at[slot])
        for i in range(min(ahead, nchunk)):
            copy(i).start()
        for i in range(nchunk):
            if i + ahead < nchunk:
                copy(i + ahead).start()
            copy(i).wait()
            dst_ref[i * chunk_rows:(i + 1) * chunk_rows, :] = stage_ref[i % WEIGHT_STAGE_SLOTS].astype(BF16)

    pl.run_scoped(run, pltpu.VMEM((WEIGHT_STAGE_SLOTS, chunk_rows, cols), F32),
                  pltpu.SemaphoreType.DMA((WEIGHT_STAGE_SLOTS,)))


def _side_cast_specs(weights, nsteps, step_of):
    in_specs, out_specs, out_shapes = [], [], []
    for w in weights:
        rows, cols = w.shape
        assert rows % (BF16_SUBLANES * nsteps) == 0, (w.shape, nsteps)
        index = lambda *ids: (step_of(*ids), 0)
        in_specs.append(pl.BlockSpec((rows // nsteps, cols), index))
        out_specs.append(pl.BlockSpec((rows // nsteps, cols), index))
        out_shapes.append(jax.ShapeDtypeStruct((rows, cols), BF16))
    return in_specs, out_specs, out_shapes


def _side_casts(side_in, side_out):
    for src, dst in zip(side_in, side_out, strict=True):
        dst[...] = src[...].astype(BF16)


def _ffn_kernel(x_ref, g_ref, w1_in, w3_in, w2_in, fg_ref, *rest, final_norm, n_side):
    side_in, (o_ref, *side_out), scratch = rest[:n_side], rest[n_side:2 * n_side + 1], rest[2 * n_side + 1:]
    _side_casts(side_in, side_out)
    if scratch:
        w1_ref, w3_ref, w2_ref = scratch

        @pl.when(pl.program_id(0) == 0)
        def _load_weights():
            _load_weight_bf16(w1_in, w1_ref)
            _load_weight_bf16(w3_in, w3_ref)
            _load_weight_bf16(w2_in, w2_ref)
    else:
        w1_ref, w3_ref, w2_ref = w1_in, w3_in, w2_in

    x = x_ref[...]
    h = _rms(x, g_ref[...]).astype(BF16)
    acc = jnp.zeros((x.shape[0], D_MODEL), F32)
    for j in range(D_FF // FFN_CHUNK):
        c = slice(j * FFN_CHUNK, (j + 1) * FFN_CHUNK)
        a = jnp.dot(h, w1_ref[:, c], preferred_element_type=F32)
        b = jnp.dot(h, w3_ref[:, c], preferred_element_type=F32)
        g = (a * jax.nn.sigmoid(a) * b).astype(BF16)
        acc = acc + jnp.dot(g, w2_ref[c, :], preferred_element_type=F32)
    y = x + 0.5 * acc
    if final_norm:
        y = _rms(y, fg_ref[...])
    o_ref[...] = y


def _ffn(x2d, gain, w1, w3, w2, final_gain, final_norm, side_weights=()):
    n = x2d.shape[0]
    assert n % TM_FFN == 0
    nsteps = n // TM_FFN
    row = pl.BlockSpec((TM_FFN, D_MODEL), lambda i: (i, 0))
    side_in, side_out, side_shapes = _side_cast_specs(side_weights, nsteps, lambda i: i)
    if w1.dtype == BF16:
        w_specs = [_resident(w1.shape), _resident(w3.shape), _resident(w2.shape)]
        scratch = []
    else:
        w_specs = [_HBM, _HBM, _HBM]
        scratch = [pltpu.VMEM(w1.shape, BF16), pltpu.VMEM(w3.shape, BF16), pltpu.VMEM(w2.shape, BF16)]
    return pl.pallas_call(
        functools.partial(_ffn_kernel, final_norm=final_norm, n_side=len(side_weights)),
        grid=(nsteps,),
        in_specs=[row, _resident((1, D_MODEL))] + w_specs + [_resident((1, D_MODEL))] + side_in,
        out_specs=[row] + side_out,
        out_shape=[jax.ShapeDtypeStruct((n, D_MODEL), F32)] + side_shapes,
        scratch_shapes=scratch,
        compiler_params=pltpu.CompilerParams(dimension_semantics=("arbitrary",), vmem_limit_bytes=VMEM_LIMIT),
        name="ffn_final" if final_norm else "ffn",
    )(x2d, gain, w1, w3, w2, final_gain, *side_weights)


A_WIDTH = 3 * GW_A
QSCALE_A = HD_A ** -0.5 * LOG2E
OFF_QB = N_GROUPS_A * A_WIDTH
OFF_KB = OFF_QB + N_HEADS_B * HD_B
OFF_VB = OFF_KB + N_KV_B * HD_B
OFF_GATE = OFF_VB + N_KV_B * HD_B
IN_WIDTH = OFF_GATE + 2 * D_MODEL


def _proj_kernel(x_ref, g_ref, w_ref, bg_ref, qn_ref, kn_ref, cos_ref, sin_ref,
                 a0_ref, a1_ref, a2_ref, qb_ref, kb_ref, vt_ref, gate_ref, hs_ref, hb_ref):
    tm = x_ref.shape[0]
    nslab = D_MODEL // LANES
    even_lane = (lax.broadcasted_iota(jnp.int32, (tm, HD_B), 1) & 1) == 0

    hn = _rms(x_ref[...], g_ref[...])
    hb_ref[0] = hn.astype(BF16)
    for s in range(nslab):
        hs_ref[0, s] = hn[:, s * LANES:(s + 1) * LANES]
    d_prev = 1
    for g in range(1, N_GROUPS_A):
        d = DILATED_GROUPS[g][1]
        f, lp, ln = d // d_prev, tm // d_prev, tm // d
        for r_prev in range(d_prev):
            for q in range(f):
                r = r_prev + d_prev * q
                for s in range(nslab):
                    rows = hs_ref[(g - 1) % 2, s, pl.ds(r_prev * lp + q, ln, stride=f), :]
                    hb_ref[g, r * ln:(r + 1) * ln, s * LANES:(s + 1) * LANES] = rows.astype(BF16)
                    if g + 1 < N_GROUPS_A:
                        hs_ref[g % 2, s, r * ln:(r + 1) * ln, :] = rows
        d_prev = d

    def mm(lo, g=0):
        return jnp.dot(hb_ref[g], w_ref[:, lo:lo + PROJ_CHUNK], preferred_element_type=F32)

    def _rope(y, cosf, sinf):
        partner = jnp.where(even_lane, pltpu.roll(y, HD_B - 1, axis=1), pltpu.roll(y, 1, axis=1))
        return y * cosf + partner * sinf

    def a_chunk(g, j):
        part, half = divmod(j, GW_A // PROJ_CHUNK)
        p = mm((part * N_GROUPS_A + g) * GW_A + half * PROJ_CHUNK, g)
        return p * QSCALE_A if part == 0 else p

    for j in range(2 * D_MODEL // PROJ_CHUNK):
        c = slice(j * PROJ_CHUNK, (j + 1) * PROJ_CHUNK)
        pg = mm(OFF_GATE + j * PROJ_CHUNK) + bg_ref[:, c]
        gate_ref[:, c] = (0.5 * jnp.tanh(0.5 * pg) + 0.5).astype(BF16)

    cosf = cos_ref[...]
    sinf = sin_ref[...]
    heads_per_chunk = PROJ_CHUNK // HD_B
    for j in range(N_HEADS_B // heads_per_chunk):
        pq = mm(OFF_QB + j * PROJ_CHUNK)
        for hh in range(heads_per_chunk):
            src = slice(hh * HD_B, (hh + 1) * HD_B)
            dst = slice(j * PROJ_CHUNK + hh * HD_B, j * PROJ_CHUNK + (hh + 1) * HD_B)
            qb_ref[:, dst] = _rope(_rms(pq[:, src], qn_ref[...]), cosf, sinf).astype(BF16)
    pk = mm(OFF_KB)
    for hh in range(N_KV_B):
        c = slice(hh * HD_B, (hh + 1) * HD_B)
        kb_ref[:, c] = _rope(_rms(pk[:, c], kn_ref[...]), cosf, sinf).astype(BF16)
    vt_ref[...] = mm(OFF_VB).T.astype(BF16)

    for g, a_ref in ((2, a2_ref), (1, a1_ref)):
        d = DILATED_GROUPS[g][1]
        for j in range(A_WIDTH // PROJ_CHUNK):
            c = slice(j * PROJ_CHUNK, (j + 1) * PROJ_CHUNK)
            a_ref[:, :, c] = a_chunk(g, j).reshape(d, tm // d, PROJ_CHUNK).astype(BF16)
    for j in range(A_WIDTH // PROJ_CHUNK):
        c = slice(j * PROJ_CHUNK, (j + 1) * PROJ_CHUNK)
        a0_ref[:, c] = a_chunk(0, j).astype(BF16)


def _proj(x1, gain, w_in, b_gate, qn, kn, cosf, sinf):
    bsz, seq, _ = x1.shape
    nt = seq // TM
    outs, specs = [], []
    for _, d in DILATED_GROUPS:
        outs.append(jax.ShapeDtypeStruct((bsz, d, seq // d, A_WIDTH), BF16))
        specs.append(pl.BlockSpec((None, d, TM // d, A_WIDTH), lambda b, t: (b, 0, t, 0)))
    outs += [jax.ShapeDtypeStruct((bsz, seq, N_HEADS_B * HD_B), BF16),
             jax.ShapeDtypeStruct((bsz, seq, N_KV_B * HD_B), BF16),
             jax.ShapeDtypeStruct((bsz, N_KV_B * HD_B, seq), BF16),
             jax.ShapeDtypeStruct((bsz, seq, 2 * D_MODEL), BF16)]
    specs += [pl.BlockSpec((None, TM, N_HEADS_B * HD_B), lambda b, t: (b, t, 0)),
              pl.BlockSpec((None, TM, N_KV_B * HD_B), lambda b, t: (b, t, 0)),
              pl.BlockSpec((None, N_KV_B * HD_B, TM), lambda b, t: (b, 0, t)),
              pl.BlockSpec((None, TM, 2 * D_MODEL), lambda b, t: (b, t, 0))]
    specs[0] = pl.BlockSpec((None, None, TM, A_WIDTH), lambda b, t: (b, 0, t, 0))
    return pl.pallas_call(
        _proj_kernel,
        grid=(bsz, nt),
        in_specs=[pl.BlockSpec((None, TM, D_MODEL), lambda b, t: (b, t, 0)),
                  _resident((1, D_MODEL)), _resident((D_MODEL, IN_WIDTH)), _resident((1, 2 * D_MODEL)),
                  _resident((1, HD_B)), _resident((1, HD_B)),
                  pl.BlockSpec((TM, HD_B), lambda b, t: (t, 0)),
                  pl.BlockSpec((TM, HD_B), lambda b, t: (t, 0))],
        out_specs=specs,
        out_shape=outs,
        scratch_shapes=[pltpu.VMEM((2, D_MODEL // LANES, TM, LANES), F32),
                        pltpu.VMEM((N_GROUPS_A, TM, D_MODEL), BF16)],
        compiler_params=pltpu.CompilerParams(dimension_semantics=("arbitrary", "arbitrary"),
                                             vmem_limit_bytes=VMEM_LIMIT),
        name="proj",
    )(x1, gain, w_in, b_gate, qn, kn, cosf, sinf)


def _dilated_kernel(bucket_ref, tab_ref, q_ref, k_ref, v_ref, o_ref, st_ref, bias_ref, s_ref, *, sub_len):
    nblk = sub_len // QBLK_A
    nunit = q_ref.shape[0] * nblk
    npair = HEADS_A // 2

    @pl.when((pl.program_id(0) == 0) & (pl.program_id(1) == 0))
    def _build_bias():
        for v in range(3):
            bt = bucket_ref[v]
            accs = [jnp.full((QBLK_A, KWIN_A), NEG_INF, F32)] * HEADS_A
            for j in range(N_BUCKETS):
                hit = bt == j
                accs = [jnp.where(hit, tab_ref[j, hh] * LOG2E, acc) for hh, acc in enumerate(accs)]
            for hh, acc in enumerate(accs):
                bias_ref[v, hh // 2, (hh % 2) * QBLK_A:(hh % 2 + 1) * QBLK_A, :] = acc

    lane = lax.broadcasted_iota(jnp.int32, (QBLK_A, LANES), 1)
    low = lane < HD_A

    def geometry(u):
        r = u >> (nblk.bit_length() - 1)
        n = u & (nblk - 1)
        q0 = pl.multiple_of(n * QBLK_A, QBLK_A)
        w0 = pl.multiple_of(jnp.clip(n * QBLK_A - (KWIN_A - QBLK_A) // 2, 0, sub_len - KWIN_A), 64)
        variant = jnp.where(n == 0, 0, jnp.where(n == nblk - 1, 2, 1))
        return r, q0, w0, variant

    def score_pass(u, pair):
        r, q0, w0, variant = geometry(u)
        c = slice(pair * LANES, (pair + 1) * LANES)
        qp = q_ref[r, pl.ds(q0, QBLK_A), c]
        kp = k_ref[r, pl.ds(w0, KWIN_A), c]
        zero = jnp.zeros_like(qp)
        q2 = jnp.concatenate([jnp.where(low, qp, zero), jnp.where(low, zero, qp)], axis=0)
        s = lax.dot_general(q2, kp, (((1,), (1,)), ((), ())), preferred_element_type=F32)
        s = s + bias_ref[variant, pair]
        s_ref[pair] = s
        return jnp.max(s, axis=-1, keepdims=True)

    def value_pass(u, pair, m, stats):
        r, q0, w0, _ = geometry(u)
        c = slice(pair * LANES, (pair + 1) * LANES)
        vp = v_ref[r, pl.ds(w0, KWIN_A), c]
        p = jnp.exp2(s_ref[pair] - m).astype(BF16)
        ve = jnp.concatenate([vp, jnp.ones_like(vp)], axis=1)
        oe = jnp.dot(p, ve, preferred_element_type=F32)
        o, l = oe[:, :LANES], oe[:, LANES:]
        o_ref[r, pl.ds(q0, QBLK_A), c] = jnp.where(low, o[:QBLK_A], o[QBLK_A:]).astype(BF16)
        for half in range(2):
            rows = slice(half * QBLK_A, (half + 1) * QBLK_A)
            hh = 2 * pair + half
            stats = jnp.where(lane == hh, m[rows], stats)
            stats = jnp.where(lane == HEADS_A + hh, l[rows], stats)
        return stats

    def body(u, ms):
        stats = jnp.zeros((QBLK_A, LANES), F32)
        for pair in range(npair):
            if pair + 1 < npair:
                nxt = score_pass(u, pair + 1)
            else:
                nxt = score_pass(jnp.minimum(u + 1, nunit - 1), 0)
            stats = value_pass(u, pair, ms, stats)
            ms = nxt
        r, q0, _, _ = geometry(u)
        st_ref[r, pl.ds(q0, QBLK_A), :] = stats
        return ms

    lax.fori_loop(0, nunit, body, score_pass(0, 0), unroll=DILATED_UNROLL)


def _dilated(a_g, buckets, table, g):
    bsz, d, sub_len, _ = a_g.shape
    nblk = sub_len // QBLK_A
    assert sub_len % QBLK_A == 0 and sub_len >= KWIN_A and nblk & (nblk - 1) == 0
    rb = min(d, max(1, DILATED_ROWS // sub_len))
    assert d % rb == 0
    blk = lambda col: pl.BlockSpec((None, rb, sub_len, GW_A), lambda b, r: (b, r, 0, col))
    return pl.pallas_call(
        functools.partial(_dilated_kernel, sub_len=sub_len),
        grid=(bsz, d // rb),
        in_specs=[_resident((3, QBLK_A, KWIN_A)),
                  pl.BlockSpec(memory_space=pltpu.SMEM),
                  blk(0), blk(1), blk(2)],
        out_specs=[blk(0), pl.BlockSpec((None, rb, sub_len, LANES), lambda b, r: (b, r, 0, 0))],
        out_shape=[jax.ShapeDtypeStruct((bsz, d, sub_len, GW_A), BF16),
                   jax.ShapeDtypeStruct((bsz, d, sub_len, LANES), F32)],
        scratch_shapes=[pltpu.VMEM((3, HEADS_A // 2, 2 * QBLK_A, KWIN_A), F32),
                        pltpu.VMEM((HEADS_A // 2, 2 * QBLK_A, KWIN_A), F32)],
        compiler_params=pltpu.CompilerParams(dimension_semantics=("arbitrary", "arbitrary"),
                                             vmem_limit_bytes=VMEM_LIMIT),
        name=f"dilated{g}",
    )(buckets, table, a_g, a_g, a_g)


def _gqa_kernel(q_ref, k_ref, vt_ref, *rest, n_side):
    side_in, (o_ref, *side_out), (s_ref, p_ref) = rest[:n_side], rest[n_side:2 * n_side + 1], rest[2 * n_side + 1:]
    _side_casts(side_in, side_out)
    seq = k_ref.shape[0]
    nch = seq // KCH_B
    nqt = seq // TQ_B

    def score_pass(qt, g):
        q0 = pl.multiple_of(qt * TQ_B, TQ_B)
        q = q_ref[pl.ds(q0, TQ_B), g * HD_B:(g + 1) * HD_B]
        mrun = jnp.full((F32_SUBLANES, TQ_B), NEG_INF, F32)
        for ch in range(nch):
            rows = slice(ch * KCH_B, (ch + 1) * KCH_B)
            st = lax.dot_general(k_ref[rows, :], q, (((1,), (1,)), ((), ())),
                                 preferred_element_type=F32)
            s_ref[g, rows, :] = st
            mrun = jnp.maximum(mrun, jnp.max(st.reshape(KCH_B // F32_SUBLANES, F32_SUBLANES, TQ_B), axis=0))
        return mrun

    def value_pass(qt, g, mrun):
        q0 = pl.multiple_of(qt * TQ_B, TQ_B)
        m = jnp.max(mrun, axis=0, keepdims=True)
        lrun = jnp.zeros((F32_SUBLANES, TQ_B), F32)
        for ch in range(nch):
            rows = slice(ch * KCH_B, (ch + 1) * KCH_B)
            p = jnp.exp2(s_ref[g, rows, :] - m)
            lrun = lrun + jnp.sum(p.reshape(KCH_B // F32_SUBLANES, F32_SUBLANES, TQ_B), axis=0)
            p_ref[g, rows, :] = p.astype(BF16)
        l = jnp.sum(lrun, axis=0, keepdims=True)
        ot = jnp.dot(vt_ref[...], p_ref[g], preferred_element_type=F32)
        o_ref[pl.ds(q0, TQ_B), g * HD_B:(g + 1) * HD_B] = (ot / l).T.astype(BF16)

    def body(qt, mrun):
        for g in range(GQA_B):
            if g + 1 < GQA_B:
                nxt = score_pass(qt, g + 1)
            else:
                nxt = score_pass(jnp.minimum(qt + 1, nqt - 1), 0)
            value_pass(qt, g, mrun)
            mrun = nxt
        return mrun

    lax.fori_loop(0, nqt, body, score_pass(0, 0), unroll=GQA_UNROLL)


def _gqa(qb, kb, vt, side_weights=()):
    bsz, seq, _ = qb.shape
    gw = GQA_B * HD_B
    side_in, side_out, side_shapes = _side_cast_specs(side_weights, bsz * N_KV_B, lambda b, kv: b * N_KV_B + kv)
    return pl.pallas_call(
        functools.partial(_gqa_kernel, n_side=len(side_weights)),
        grid=(bsz, N_KV_B),
        in_specs=[pl.BlockSpec((None, seq, gw), lambda b, kv: (b, 0, kv)),
                  pl.BlockSpec((None, seq, HD_B), lambda b, kv: (b, 0, kv)),
                  pl.BlockSpec((None, HD_B, seq), lambda b, kv: (b, kv, 0))] + side_in,
        out_specs=[pl.BlockSpec((None, seq, gw), lambda b, kv: (b, 0, kv))] + side_out,
        out_shape=[jax.ShapeDtypeStruct((bsz, seq, N_HEADS_B * HD_B), BF16)] + side_shapes,
        scratch_shapes=[pltpu.VMEM((GQA_B, seq, TQ_B), F32), pltpu.VMEM((GQA_B, seq, TQ_B), BF16)],
        compiler_params=pltpu.CompilerParams(dimension_semantics=("arbitrary", "arbitrary"),
                                             vmem_limit_bytes=VMEM_LIMIT),
        name="gqa",
    )(qb, kb, vt, *side_weights)


def _merge_kernel(x_ref, gate_ref, ob_ref, o0_ref, s0_ref, o1_ref, s1_ref, o2_ref, s2_ref,
                  wa_ref, wb_ref, wo_ref, expand_ref, out_ref, oslab_ref, sslab_ref, tmp_ref):
    tm = x_ref.shape[0]
    nslab = GW_A // LANES

    def planes(g, src_o, src_s, dst_o, dst_s, lp):
        d = DILATED_GROUPS[g][1]
        d_dst = d // 4
        for r_dst in range(d_dst):
            for q in range(4):
                rows = pl.ds(r_dst * 4 * lp + q, lp, stride=4)
                dst_s(rows, src_s(r_dst + d_dst * q))
                for s in range(nslab):
                    dst_o(s, rows, src_o(r_dst + d_dst * q, s))

    lane_blk = lambda s: slice(s * LANES, (s + 1) * LANES)

    def put_tmp_o(s, rows, val):
        tmp_ref[s, rows, :] = val

    def put_tmp_s(rows, val):
        tmp_ref[nslab, rows, :] = val

    def put_slab_o(g):
        def put(s, rows, val):
            oslab_ref[g, s, rows, :] = val
        return put

    def put_slab_s(g):
        def put(rows, val):
            sslab_ref[g, rows, :] = val
        return put

    l1, l2 = tm // DILATED_GROUPS[1][1], tm // DILATED_GROUPS[2][1]
    planes(1, lambda r, s: o1_ref[r, :, lane_blk(s)].astype(F32), lambda r: s1_ref[r],
           put_slab_o(0), put_slab_s(0), l1)
    planes(2, lambda r, s: o2_ref[r, :, lane_blk(s)].astype(F32), lambda r: s2_ref[r],
           put_tmp_o, put_tmp_s, l2)
    planes(1, lambda r, s: tmp_ref[s, r * l1:(r + 1) * l1, :], lambda r: tmp_ref[nslab, r * l1:(r + 1) * l1, :],
           put_slab_o(1), put_slab_s(1), l1)

    stats = [s0_ref[...], sslab_ref[0], sslab_ref[1]]
    dens = [pltpu.roll(st, LANES - HEADS_A, axis=1) for st in stats]
    mx = jnp.maximum(jnp.maximum(stats[0], stats[1]), stats[2])
    es = [jnp.exp2(st - mx) for st in stats]
    total = es[0] * dens[0] + es[1] * dens[1] + es[2] * dens[2]
    head_lane = lax.broadcasted_iota(jnp.int32, (tm, LANES), 1) < HEADS_A
    o_a = jnp.zeros((tm, GW_A), F32)
    for g in range(N_GROUPS_A):
        w = jnp.where(head_lane, es[g] / total, 0.0)
        w_hi = w.astype(BF16)
        w_lo = (w - w_hi.astype(F32)).astype(BF16)
        wide = jnp.dot(jnp.concatenate([w_hi, w_lo], axis=1), expand_ref[...], preferred_element_type=F32)
        if g == 0:
            acc = o0_ref[...].astype(F32)
        else:
            acc = jnp.concatenate([oslab_ref[g - 1, s] for s in range(nslab)], axis=1)
        o_a = o_a + wide * acc
    o_a = o_a.astype(BF16)

    ya = jnp.dot(o_a, wa_ref[...], preferred_element_type=F32)
    yb = jnp.dot(ob_ref[...], wb_ref[...], preferred_element_type=F32)
    merged = (gate_ref[:, :D_MODEL].astype(F32) * ya + gate_ref[:, D_MODEL:].astype(F32) * yb).astype(BF16)
    out_ref[...] = x_ref[...] + jnp.dot(merged, wo_ref[...], preferred_element_type=F32)


def _merge(x1, gates, ob, outs, stats, wa, wb, wo):
    bsz, seq, _ = x1.shape
    tok = lambda w: pl.BlockSpec((None, TM, w), lambda b, t: (b, t, 0))
    a_specs = []
    for _, d in DILATED_GROUPS:
        for width in (GW_A, LANES):
            if d == 1:
                a_specs.append(pl.BlockSpec((None, None, TM, width), lambda b, t: (b, 0, t, 0)))
            else:
                a_specs.append(pl.BlockSpec((None, d, TM // d, width), lambda b, t: (b, 0, t, 0)))
    a_args = [arr for pair in zip(outs, stats) for arr in pair]
    src = jnp.arange(2 * LANES, dtype=jnp.int32)[:, None] % LANES
    dst = jnp.arange(GW_A, dtype=jnp.int32)[None, :] // HD_A
    expand = (src == dst).astype(BF16)
    return pl.pallas_call(
        _merge_kernel,
        grid=(bsz, seq // TM),
        in_specs=[tok(D_MODEL), tok(2 * D_MODEL), tok(N_HEADS_B * HD_B)] + a_specs
                 + [_resident((GW_A, D_MODEL)), _resident((N_HEADS_B * HD_B, D_MODEL)),
                    _resident((D_MODEL, D_MODEL)), _resident((2 * LANES, GW_A))],
        out_specs=tok(D_MODEL),
        out_shape=jax.ShapeDtypeStruct((bsz, seq, D_MODEL), F32),
        scratch_shapes=[pltpu.VMEM((2, GW_A // LANES, TM, LANES), F32),
                        pltpu.VMEM((2, TM, LANES), F32),
                        pltpu.VMEM((GW_A // LANES + 1, TM, LANES), F32)],
        compiler_params=pltpu.CompilerParams(dimension_semantics=("arbitrary", "arbitrary"),
                                             vmem_limit_bytes=VMEM_LIMIT),
        name="merge",
    )(x1, gates, ob, *a_args, wa, wb, wo, expand)


def _t5_bucket(rel):
    n = N_BUCKETS // 2
    max_exact = n // 2
    ret = jnp.where(rel > 0, n, 0)
    a = jnp.abs(rel)
    af = jnp.maximum(a, 1).astype(F32)
    large = max_exact + (jnp.log(af / max_exact) / math.log(MAX_DISTANCE / max_exact)
                         * (n - max_exact)).astype(jnp.int32)
    large = jnp.minimum(large, n - 1)
    return ret + jnp.where(a < max_exact, a, large)


def _bucket_tiles(dilation, half):
    qi = jnp.arange(QBLK_A, dtype=jnp.int32)[:, None]
    kj = jnp.arange(KWIN_A, dtype=jnp.int32)[None, :]
    tiles = []
    for shift in (0, (KWIN_A - QBLK_A) // 2, KWIN_A - QBLK_A):
        rel = kj - shift - qi
        tiles.append(jnp.where(jnp.abs(rel) <= half, _t5_bucket(rel * dilation), -1))
    return jnp.stack(tiles)


def _rope_tables(seq):
    t = np.arange(seq)
    n_freq = HD_B // 4
    freq = (ROPE_THETA ** (-np.arange(n_freq, dtype=np.float32) / n_freq)).astype(np.float32)
    ang = np.concatenate([(t // GRID_W).astype(np.float32)[:, None] * freq,
                          (t % GRID_W).astype(np.float32)[:, None] * freq], -1)
    cos, sin = np.cos(ang), np.sin(ang)
    cosf = np.repeat(cos, 2, axis=-1)
    sinf = np.stack([-sin, sin], axis=-1).reshape(seq, HD_B)
    return jnp.asarray(cosf, F32), jnp.asarray(sinf, F32)


def kernel(x, ffn1_norm, ffn1_w1, ffn1_w3, ffn1_w2, mix_norm, w_in, b_gate, q_norm, k_norm, rel_bias,
           w_branch_a, w_branch_b, w_out, ffn2_norm, ffn2_w1, ffn2_w3, ffn2_w2, final_norm):
    bsz, seq, dm = x.shape
    assert dm == D_MODEL and seq % TM == 0 and ffn1_w1.shape[0] == 1
    row = lambda v: v.reshape(1, -1).astype(F32)
    cosf, sinf = _rope_tables(seq)
    qn = row(q_norm[0] * (HD_B ** -0.5 * LOG2E))
    kn = row(k_norm[0])

    x1, w_in_bf = _ffn(x.reshape(bsz * seq, dm), row(ffn1_norm[0]), ffn1_w1[0], ffn1_w3[0], ffn1_w2[0],
                       row(final_norm), False, side_weights=(w_in[0],))
    x1 = x1.reshape(bsz, seq, dm)
    a0, a1, a2, qb, kb, vt, gates = _proj(x1, row(mix_norm[0]), w_in_bf, row(b_gate[0]), qn, kn, cosf, sinf)

    outs, stats = [], []
    for g, (a_g, (window, d)) in enumerate(zip((a0, a1, a2), DILATED_GROUPS)):
        table = rel_bias[:, g * HEADS_A:(g + 1) * HEADS_A].astype(F32)
        o_g, st_g = _dilated(a_g, _bucket_tiles(d, window // (2 * d)), table, g)
        outs.append(o_g)
        stats.append(st_g)
    ob, wa, wb, wo, w1, w3, w2 = _gqa(qb, kb, vt, side_weights=(
        w_branch_a[0], w_branch_b[0], w_out[0], ffn2_w1[0], ffn2_w3[0], ffn2_w2[0]))

    x2 = _merge(x1, gates, ob, outs, stats, wa, wb, wo)
    y, = _ffn(x2.reshape(bsz * seq, dm), row(ffn2_norm[0]), w1, w3, w2, row(final_norm), True)
    return y.reshape(bsz, seq, dm)
```
